```python
import math
import jax, jax.numpy as jnp
from jax import lax
import numpy as np

D_MODEL = 4096
BATCH = 2
SEQ = 8192
DEPTH = 4

GRID_W = 64
CTX_LEN = 256
ADA_RANK = 256
N_MOD = 6
A_HEADS = 8
A_DIM = 128
B_HEADS = 8
B_DK = 128
B_DV = 256
C_GROUPS = 4
C_GROUP_DIM = 512
FFN_DIM = 5 * D_MODEL // 4
Q_BLOCK = 128
RET_CHUNK = 128
ROPE_BASE = 10000.0
NORM_EPS = 1e-6

A_QK = A_HEADS * 2 * A_DIM
A_V = A_HEADS * 2 * A_DIM
B_QK = B_HEADS * B_DK
B_V = B_HEADS * B_DV
C_W = C_GROUPS * C_GROUP_DIM
N_BRANCH = 3
PROJ_SIZES = (A_QK, A_V, B_QK, B_V, A_QK, B_QK, B_V, C_W, N_BRANCH * D_MODEL)
KV_COLS = A_QK + A_V + B_QK + B_V
IN_COLS = KV_COLS + A_QK + B_QK + B_V + C_W + N_BRANCH * D_MODEL

kernel_name = "hybrid_flow_backbone_diffattn_retnet_fnet"


def rmsnorm(x, g):
    xf = x.astype(jnp.float32)
    y = xf * lax.rsqrt(jnp.mean(xf * xf, axis=-1, keepdims=True) + NORM_EPS)
    return (y * g.astype(jnp.float32)).astype(x.dtype)


def modulate(h, shift, scale):
    return h * (1 + scale) + shift


def ada_params(cond, w_down, w_up, b):
    m = (jax.nn.silu(cond) @ w_down) @ w_up + b
    return m.reshape(m.shape[:-1] + (N_MOD, D_MODEL))


def axial_rope_tables(n_tokens):
    rows = n_tokens // GRID_W
    r, col = jnp.meshgrid(jnp.arange(rows), jnp.arange(GRID_W), indexing="ij")
    pos = jnp.stack([r.reshape(-1), col.reshape(-1)], axis=-1).astype(jnp.float32)
    n_freq = A_DIM // 4
    inv_freq = ROPE_BASE ** (-jnp.arange(n_freq, dtype=jnp.float32) / n_freq)
    ang = pos[:, :, None, None] * inv_freq
    ang = jnp.broadcast_to(ang, (n_tokens, 2, 2, n_freq)).reshape(n_tokens, A_DIM)
    return jnp.cos(ang), jnp.sin(ang)


def apply_axial_rope(x, cos, sin):
    shape = (1, x.shape[1]) + (1,) * (x.ndim - 3) + (x.shape[-1],)
    cos = cos.reshape(shape)
    sin = sin.reshape(shape)
    xs = x.reshape(x.shape[:-1] + (2, 2, x.shape[-1] // 4))
    rot = jnp.concatenate([-xs[..., 1:, :], xs[..., :1, :]], axis=-2).reshape(x.shape)
    return (x * cos + rot * sin).astype(x.dtype)


def _split(p, sizes):
    return jnp.split(p, np.cumsum(np.array(sizes))[:-1].tolist(), axis=-1)


def _heads(t, n_heads, d):
    return t.reshape(t.shape[:2] + (n_heads, d))


def _heads_a(t):
    return t.reshape(t.shape[:2] + (A_HEADS, 2, A_DIM))


def diff_attend(q, k, v, lam):
    s = jnp.einsum("bqhmd,bkhmd->bhmqk", q, k) * (A_DIM ** -0.5)
    p = jax.nn.softmax(s.astype(jnp.float32), axis=-1)
    a = p[:, :, 0] - lam * p[:, :, 1]
    return jnp.einsum("bhqk,bkhe->bqhe", a.astype(v.dtype), v)


def diff_attend_latent(q, k_all, v_all, lam):
    B, L = q.shape[:2]
    qb = q.reshape((B, L // Q_BLOCK, Q_BLOCK) + q.shape[2:]).swapaxes(0, 1)
    ob = lax.map(lambda blk: diff_attend(blk, k_all, v_all, lam), qb)
    return ob.swapaxes(0, 1).reshape(B, L, A_HEADS, 2 * A_DIM)


def retention_chunked(q, k, v, log_g, s0):
    B, L, H, dk = q.shape
    dv = v.shape[-1]
    N = L // RET_CHUNK
    qc = q.reshape(B, N, RET_CHUNK, H, dk)
    kc = k.reshape(B, N, RET_CHUNK, H, dk)
    vc = v.reshape(B, N, RET_CHUNK, H, dv)
    pos = jnp.arange(RET_CHUNK, dtype=jnp.float32)
    rel = pos[:, None] - pos[None, :]
    d_intra = jnp.where(rel >= 0, jnp.exp(log_g[:, None, None] * jnp.maximum(rel, 0.0)), 0.0)
    scores = jnp.einsum("bnqhd,bnkhd->bnhqk", qc, kc) * d_intra
    intra = jnp.einsum("bnhqk,bnkhe->bnqhe", scores, vc)
    k_dec = kc * jnp.exp(log_g * (RET_CHUNK - 1 - pos)[:, None])[:, :, None]
    chunk_kv = jnp.einsum("bnkhd,bnkhe->nbhde", k_dec, vc)
    g_chunk = jnp.exp(log_g * RET_CHUNK)[None, :, None, None]

    def step(s, kv):
        return g_chunk * s + kv, s

    s_final, s_prev = lax.scan(step, s0, chunk_kv)
    q_dec = qc * jnp.exp(log_g * (pos + 1)[:, None])[:, :, None]
    inter = jnp.einsum("bnqhd,nbhde->bnqhe", q_dec, s_prev)
    out = (intra + inter).reshape(B, L, H, dv).astype(v.dtype)
    return out, s_final


def retention_final_state(k, v, log_g):
    L = k.shape[1]
    w = jnp.exp(log_g * (L - 1 - jnp.arange(L, dtype=jnp.float32))[:, None])
    return jnp.einsum("blhd,blhe->bhde", k * w[:, :, None], v)


def fourier_mix(u):
    B, L, _ = u.shape
    ug = u.reshape(B, L, C_GROUPS, C_GROUP_DIM).astype(jnp.float32)
    f = jnp.fft.fft2(ug, axes=(1, 3), norm="ortho").real
    return f.reshape(B, L, C_W).astype(u.dtype)


def _attn_post(o, subln, lam_init):
    return (rmsnorm(o, subln) * (1.0 - lam_init)).reshape(o.shape[:2] + (A_V,))


def _ret_post(o, gate, subln):
    g = jax.nn.silu(_heads(gate, B_HEADS, B_DV))
    return (rmsnorm(o, subln) * g).reshape(o.shape[:2] + (B_V,))


def _merge(oa, ob, oc, gates, w_pa, w_pb, w_pc, w_out):
    g = jax.nn.sigmoid(gates.reshape(gates.shape[:2] + (N_BRANCH, D_MODEL)))
    y = g[:, :, 0] * (oa @ w_pa) + g[:, :, 1] * (ob @ w_pb) + g[:, :, 2] * (oc @ w_pc)
    return y @ w_out


def parallel_mixer(h, hc, cos, sin, w_in, lam, lam_init, attn_subln, log_f, log_b, ret_subln,
                   w_pa, w_pb, w_pc, w_out, with_ctx_out):
    B = h.shape[0]
    ka, va, kb, vb, qa, qb, gb, uc, gates = _split(h @ w_in, PROJ_SIZES)
    if with_ctx_out:
        ka_c, va_c, kb_c, vb_c, qa_c, qb_c, gb_c, uc_c, gates_c = _split(hc @ w_in, PROJ_SIZES)
    else:
        ka_c, va_c, kb_c, vb_c = _split(hc @ w_in[:, :KV_COLS], PROJ_SIZES[:4])

    ka_c = _heads_a(ka_c)
    va_c = _heads(va_c, A_HEADS, 2 * A_DIM)
    k_all = jnp.concatenate([ka_c, apply_axial_rope(_heads_a(ka), cos, sin)], axis=1)
    v_all = jnp.concatenate([va_c, _heads(va, A_HEADS, 2 * A_DIM)], axis=1)
    oa = _attn_post(diff_attend_latent(apply_axial_rope(_heads_a(qa), cos, sin), k_all, v_all, lam),
                    attn_subln, lam_init)

    kscale = B_DK ** -0.5
    kb_c = _heads(kb_c, B_HEADS, B_DK) * kscale
    vb_c = _heads(vb_c, B_HEADS, B_DV)
    if with_ctx_out:
        qb_c = _heads(qb_c, B_HEADS, B_DK)
        zero = jnp.zeros((B, B_HEADS, B_DK, B_DV), jnp.float32)
        oc_f, s_f = retention_chunked(qb_c, kb_c, vb_c, log_f, zero)
        oc_b, s_b = retention_chunked(qb_c[:, ::-1], kb_c[:, ::-1], vb_c[:, ::-1], log_b, zero)
        ob_c = oc_f + oc_b[:, ::-1]
    else:
        s_f = retention_final_state(kb_c, vb_c, log_f)
        s_b = retention_final_state(kb_c[:, ::-1], vb_c[:, ::-1], log_b)
    qb_l = apply_axial_rope(_heads(qb, B_HEADS, B_DK), cos, sin)
    kb_l = apply_axial_rope(_heads(kb, B_HEADS, B_DK), cos, sin) * kscale
    vb_l = _heads(vb, B_HEADS, B_DV)
    o_f, _ = retention_chunked(qb_l, kb_l, vb_l, log_f, s_f)
    o_b, _ = retention_chunked(qb_l[:, ::-1], kb_l[:, ::-1], vb_l[:, ::-1], log_b, s_b)
    ob = _ret_post(o_f + o_b[:, ::-1], gb, ret_subln)

    oc = fourier_mix(uc)

    y = _merge(oa, ob, oc, gates, w_pa, w_pb, w_pc, w_out)
    if not with_ctx_out:
        return y, None
    oa_c = _attn_post(diff_attend(qa_c.reshape(qa_c.shape[:2] + (A_HEADS, 2, A_DIM)), ka_c, va_c, lam),
                      attn_subln, lam_init)
    ob_c = _ret_post(ob_c, gb_c, ret_subln)
    oc_c = fourier_mix(uc_c)
    y_c = _merge(oa_c, ob_c, oc_c, gates_c, w_pa, w_pb, w_pc, w_out)
    return y, y_c


def dwconv3(u, w, b):
    up = jnp.pad(u, ((0, 0), (1, 1), (0, 0)))
    return up[:, :-2] * w[0] + up[:, 1:-1] * w[1] + up[:, 2:] * w[2] + b


def conv_ffn(h, w_up, conv_w, conv_b, w_down):
    u = dwconv3(h @ w_up, conv_w, conv_b)
    gate, val = jnp.split(u, 2, axis=-1)
    return (jax.nn.silu(gate) * val) @ w_down


def setup_inputs(seed: int = 0) -> dict:
    key = jax.random.key(seed)
    ks = jax.random.split(key, 28)
    f32 = jnp.float32

    def nrm(k, shape, scale):
        return jax.random.normal(k, shape, f32) * scale

    gamma0 = 1.0 - 2.0 ** (-5.0 - jnp.arange(B_HEADS, dtype=f32))
    logit0 = jnp.log(gamma0) - jnp.log1p(-gamma0)
    return {
        "x": nrm(ks[0], (BATCH, SEQ, D_MODEL), 1.0),
        "c": nrm(ks[1], (BATCH, D_MODEL), 1.0),
        "ctx": nrm(ks[2], (BATCH, CTX_LEN, D_MODEL), 1.0),
        "c_ctx": nrm(ks[3], (D_MODEL,), 1.0),
        "w_ada_down": nrm(ks[4], (DEPTH, D_MODEL, ADA_RANK), D_MODEL ** -0.5),
        "w_ada_up": nrm(ks[5], (DEPTH, ADA_RANK, N_MOD * D_MODEL), ADA_RANK ** -0.5),
        "b_ada": nrm(ks[6], (DEPTH, N_MOD * D_MODEL), 0.02),
        "norm1_g": 1.0 + nrm(ks[7], (DEPTH, D_MODEL), 0.02),
        "norm2_g": 1.0 + nrm(ks[8], (DEPTH, D_MODEL), 0.02),
        "w_in": nrm(ks[9], (DEPTH, D_MODEL, IN_COLS), D_MODEL ** -0.5),
        "lam_q1": nrm(ks[10], (DEPTH, A_DIM), 0.1),
        "lam_k1": nrm(ks[11], (DEPTH, A_DIM), 0.1),
        "lam_q2": nrm(ks[12], (DEPTH, A_DIM), 0.1),
        "lam_k2": nrm(ks[13], (DEPTH, A_DIM), 0.1),
        "attn_subln": 1.0 + nrm(ks[14], (DEPTH, 2 * A_DIM), 0.02),
        "ret_decay_fwd": logit0 + nrm(ks[15], (DEPTH, B_HEADS), 0.05),
        "ret_decay_bwd": logit0 + nrm(ks[16], (DEPTH, B_HEADS), 0.05),
        "ret_subln": 1.0 + nrm(ks[17], (DEPTH, B_DV), 0.02),
        "w_pa": nrm(ks[18], (DEPTH, A_V, D_MODEL), A_V ** -0.5),
        "w_pb": nrm(ks[19], (DEPTH, B_V, D_MODEL), B_V ** -0.5),
        "w_pc": nrm(ks[20], (DEPTH, C_W, D_MODEL), C_W ** -0.5),
        "w_out": nrm(ks[21], (DEPTH, D_MODEL, D_MODEL), D_MODEL ** -0.5),
        "w_up": nrm(ks[22], (DEPTH, D_MODEL, 2 * FFN_DIM), D_MODEL ** -0.5),
        "conv_w": nrm(ks[23], (DEPTH, 3, 2 * FFN_DIM), 3 ** -0.5),
        "conv_b": nrm(ks[24], (DEPTH, 2 * FFN_DIM), 0.02),
        "w_down": nrm(ks[25], (DEPTH, FFN_DIM, D_MODEL), FFN_DIM ** -0.5),
        "final_g": 1.0 + nrm(ks[26], (D_MODEL,), 0.02),
    }


def reference(x, c, ctx, c_ctx, w_ada_down, w_ada_up, b_ada, norm1_g, norm2_g, w_in,
              lam_q1, lam_k1, lam_q2, lam_k2, attn_subln, ret_decay_fwd, ret_decay_bwd, ret_subln,
              w_pa, w_pb, w_pc, w_out, w_up, conv_w, conv_b, w_down, final_g):
    f32 = jnp.float32
    cos, sin = axial_rope_tables(x.shape[1])
    for l in range(DEPTH):
        last = l == DEPTH - 1
        lam_init = 0.8 - 0.6 * math.exp(-0.3 * l)
        lam = (jnp.exp(jnp.sum(lam_q1[l].astype(f32) * lam_k1[l].astype(f32)))
               - jnp.exp(jnp.sum(lam_q2[l].astype(f32) * lam_k2[l].astype(f32))) + lam_init)
        log_f = jax.nn.log_sigmoid(ret_decay_fwd[l].astype(f32))
        log_b = jax.nn.log_sigmoid(ret_decay_bwd[l].astype(f32))
        m = ada_params(c, w_ada_down[l], w_ada_up[l], b_ada[l])
        mc = ada_params(c_ctx, w_ada_down[l], w_ada_up[l], b_ada[l])

        h = modulate(rmsnorm(x, norm1_g[l]), m[:, None, 0], m[:, None, 1])
        hc = modulate(rmsnorm(ctx, norm1_g[l]), mc[0], mc[1])
        y, y_c = parallel_mixer(h, hc, cos, sin, w_in[l], lam, lam_init, attn_subln[l], log_f, log_b,
                                ret_subln[l], w_pa[l], w_pb[l], w_pc[l], w_out[l], not last)
        x = x + m[:, None, 2] * y
        h2 = modulate(rmsnorm(x, norm2_g[l]), m[:, None, 3], m[:, None, 4])
        x = x + m[:, None, 5] * conv_ffn(h2, w_up[l], conv_w[l], conv_b[l], w_down[l])
        if not last:
            ctx = ctx + mc[2] * y_c
            hc2 = modulate(rmsnorm(ctx, norm2_g[l]), mc[3], mc[4])
            ctx = ctx + mc[5] * conv_ffn(hc2, w_up[l], conv_w[l], conv_b[l], w_down[l])
    return rmsnorm(x, final_g)
```

```python
import functools
import math

import numpy as np
import jax
import jax.numpy as jnp
from jax import lax
from jax.experimental import pallas as pl
from jax.experimental.pallas import tpu as pltpu

F32 = jnp.float32
BF16 = jnp.bfloat16

GRID_W = 64
N_MOD = 6
A_HEADS = 8
A_DIM = 128
B_HEADS = 8
B_DK = 128
B_DV = 256
C_GROUPS = 4
C_GROUP_DIM = 512
ROPE_BASE = 10000.0
NORM_EPS = 1e-6
A_QK = A_HEADS * 2 * A_DIM
A_V = A_HEADS * 2 * A_DIM
B_QK = B_HEADS * B_DK
B_V = B_HEADS * B_DV
C_W = C_GROUPS * C_GROUP_DIM
OFF_KA = 0
OFF_VA = OFF_KA + A_QK
OFF_KB = OFF_VA + A_V
OFF_VB = OFF_KB + B_QK
OFF_QA = OFF_VB + B_V
OFF_QB = OFF_QA + A_QK
OFF_GB = OFF_QB + B_QK
OFF_UC = OFF_GB + B_V
OFF_GATES = OFF_UC + C_W
MIX_COLS = OFF_UC

V7X_LANES = 128
V7X_BF16_SUBLANES = 16
V7X_MOD_ROWS = 8
V7X_VMEM_LIMIT_BYTES = 56 * 1024 * 1024


def _cparams(*sem):
    return pltpu.CompilerParams(dimension_semantics=sem, vmem_limit_bytes=V7X_VMEM_LIMIT_BYTES)


def _pick(n, prefs):
    for p in prefs:
        if n % p == 0:
            return p
    return n


def _smem():
    return pl.BlockSpec(memory_space=pltpu.SMEM)


def _ada_kernel(cond_ref, wd_ref, wu_ref, b_ref, o_ref, t_ref):
    @pl.when((pl.program_id(1) == 0) & (pl.program_id(2) == 0))
    def _():
        cnd = cond_ref[...]
        s = cnd * jax.nn.sigmoid(cnd)
        t_ref[...] = jnp.dot(s.astype(BF16), wd_ref[...].astype(BF16), preferred_element_type=F32)

    o_ref[...] = jnp.dot(t_ref[...].astype(BF16), wu_ref[...].astype(BF16),
                         preferred_element_type=F32) + b_ref[...]


def _ada_params(cond, w_down, w_up, b):
    depth, d, r = w_down.shape
    tn = _pick(d, (2048, 1024, 512, 256, 128))
    nj = d // tn
    out = pl.pallas_call(
        _ada_kernel,
        grid=(depth, N_MOD, nj),
        in_specs=[
            pl.BlockSpec((V7X_MOD_ROWS, d), lambda l, k, j: (0, 0)),
            pl.BlockSpec((None, d, r), lambda l, k, j: (l, 0, 0)),
            pl.BlockSpec((None, r, tn), lambda l, k, j: (l, 0, k * nj + j)),
            pl.BlockSpec((None, 1, tn), lambda l, k, j: (l, 0, k * nj + j)),
        ],
        out_specs=pl.BlockSpec((None, None, V7X_MOD_ROWS, tn), lambda l, k, j: (l, k, 0, j)),
        out_shape=jax.ShapeDtypeStruct((depth, N_MOD, V7X_MOD_ROWS, d), F32),
        scratch_shapes=[pltpu.VMEM((V7X_MOD_ROWS, r), F32)],
        compiler_params=_cparams("arbitrary", "arbitrary", "arbitrary"),
        name="ada_params",
    )(cond, w_down, w_up, b.reshape(depth, 1, N_MOD * d))
    return out.reshape(depth, N_MOD, V7X_MOD_ROWS, 1, d)


class _Stream:
    def __init__(self, batch, length, is_ctx):
        self.batch = batch
        self.length = length
        self.is_ctx = is_ctx
        self.rows = batch * length

    def seg(self, tile_rows):
        if self.is_ctx:
            return lambda i: self.batch
        per = self.length // tile_rows
        return lambda i: i // per


def _norm_mod_kernel(x_ref, g_ref, sh_ref, sc_ref, o_ref):
    x = x_ref[...]
    y = x * lax.rsqrt(jnp.mean(x * x, axis=-1, keepdims=True) + NORM_EPS) * g_ref[...]
    o_ref[...] = (y * (1.0 + sc_ref[...]) + sh_ref[...]).astype(o_ref.dtype)


def _norm_kernel(x_ref, g_ref, o_ref):
    x = x_ref[...]
    y = x * lax.rsqrt(jnp.mean(x * x, axis=-1, keepdims=True) + NORM_EPS) * g_ref[...]
    o_ref[...] = y.astype(o_ref.dtype)


def _norm_mod(st, x, g, mods, k_shift):
    m, d = x.shape
    tr = _pick(st.length, (256, 128, 64, 32, 16))
    seg = st.seg(tr)
    return pl.pallas_call(
        _norm_mod_kernel,
        grid=(m // tr,),
        in_specs=[
            pl.BlockSpec((tr, d), lambda i: (i, 0)),
            pl.BlockSpec((1, d), lambda i: (0, 0)),
            pl.BlockSpec((None, None, 1, d), lambda i: (k_shift, seg(i), 0, 0)),
            pl.BlockSpec((None, None, 1, d), lambda i: (k_shift + 1, seg(i), 0, 0)),
        ],
        out_specs=pl.BlockSpec((tr, d), lambda i: (i, 0)),
        out_shape=jax.ShapeDtypeStruct((m, d), BF16),
        compiler_params=_cparams("parallel"),
        name="norm_mod",
    )(x, g.reshape(1, d), mods, mods)


def _final_norm(x, g):
    m, d = x.shape
    tr = _pick(m, (256, 128, 64, 32, 16, 8))
    return pl.pallas_call(
        _norm_kernel,
        grid=(m // tr,),
        in_specs=[pl.BlockSpec((tr, d), lambda i: (i, 0)), pl.BlockSpec((1, d), lambda i: (0, 0))],
        out_specs=pl.BlockSpec((tr, d), lambda i: (i, 0)),
        out_shape=jax.ShapeDtypeStruct((m, d), F32),
        compiler_params=_cparams("parallel"),
        name="final_norm",
    )(x, g.reshape(1, d))


def _in_range(c0, off, width):
    return (c0 >= off) & (c0 < off + width)


def _proj_kernel(h_ref, w_ref, cos_ref, sin_ref, o_ref, *, tn, col0, use_rope):
    acc = jnp.dot(h_ref[...], w_ref[...], preferred_element_type=F32)
    c0 = col0 + pl.program_id(0) * tn
    in_kb = _in_range(c0, OFF_KB, B_QK)
    in_qa = _in_range(c0, OFF_QA, A_QK)
    is_rope = _in_range(c0, OFF_KA, A_QK) | in_kb | in_qa | _in_range(c0, OFF_QB, B_QK)
    is_silu = _in_range(c0, OFF_GB, B_V)
    is_sig = c0 >= OFF_GATES
    is_plain = jnp.logical_not(is_rope | is_silu | is_sig)

    @pl.when(is_rope)
    def _():
        scale = jnp.where(in_kb, B_DK ** -0.5, jnp.where(in_qa, A_DIM ** -0.5, 1.0)).astype(F32)
        if use_rope:
            cos = cos_ref[...] * scale
            sin = sin_ref[...] * scale
            lane = lax.broadcasted_iota(jnp.int32, cos.shape, 1)
            first_half = (lane % (A_DIM // 2)) < (A_DIM // 4)
            for c in range(tn // A_DIM):
                xc = acc[:, c * A_DIM:(c + 1) * A_DIM]
                rot = jnp.where(first_half, pltpu.roll(xc, A_DIM - A_DIM // 4, 1),
                                pltpu.roll(xc, A_DIM // 4, 1))
                o_ref[:, c * A_DIM:(c + 1) * A_DIM] = (xc * cos + rot * sin).astype(o_ref.dtype)
        else:
            o_ref[...] = (acc * scale).astype(o_ref.dtype)

    @pl.when(is_silu)
    def _():
        o_ref[...] = (acc * jax.nn.sigmoid(acc)).astype(o_ref.dtype)

    @pl.when(is_sig)
    def _():
        o_ref[...] = jax.nn.sigmoid(acc).astype(o_ref.dtype)

    @pl.when(is_plain)
    def _():
        o_ref[...] = acc.astype(o_ref.dtype)


def _proj(st, h, w, col0, ncols, cos, sin_signed):
    m, k = h.shape
    tm = _pick(st.length, (1024, 512, 256, 128))
    tn = _pick(ncols, (1024, 512, 256, 128))
    per = st.length // tm
    use_rope = not st.is_ctx
    j0 = col0 // tn
    return pl.pallas_call(
        functools.partial(_proj_kernel, tn=tn, col0=col0, use_rope=use_rope),
        grid=(ncols // tn, m // tm),
        in_specs=[
            pl.BlockSpec((tm, k), lambda j, i: (i, 0)),
            pl.BlockSpec((k, tn), lambda j, i: (0, j0 + j)),
            pl.BlockSpec((tm, A_DIM), lambda j, i: (i % per, 0)),
            pl.BlockSpec((tm, A_DIM), lambda j, i: (i % per, 0)),
        ],
        out_specs=pl.BlockSpec((tm, tn), lambda j, i: (i, j)),
        out_shape=jax.ShapeDtypeStruct((m, ncols), BF16),
        compiler_params=_cparams("parallel", "parallel"),
        name="in_proj",
    )(h, w, cos, sin_signed)


def _mm_kernel(a_ref, w_ref, o_ref):
    o_ref[...] = jnp.dot(a_ref[...], w_ref[...], preferred_element_type=F32).astype(o_ref.dtype)


def _matmul(st, a, w, out_dtype):
    m, k = a.shape
    n = w.shape[1]
    tm = _pick(st.length, (1024, 512, 256, 128))
    tn = _pick(n, (1024, 512, 256, 128))
    return pl.pallas_call(
        _mm_kernel,
        grid=(n // tn, m // tm),
        in_specs=[pl.BlockSpec((tm, k), lambda j, i: (i, 0)), pl.BlockSpec((k, tn), lambda j, i: (0, j))],
        out_specs=pl.BlockSpec((tm, tn), lambda j, i: (i, j)),
        out_shape=jax.ShapeDtypeStruct((m, n), out_dtype),
        compiler_params=_cparams("parallel", "parallel"),
        name="ffn_up",
    )(a, w)


def _res_mm_kernel(a_ref, w_ref, x_ref, g_ref, o_ref):
    y = jnp.dot(a_ref[...], w_ref[...], preferred_element_type=F32)
    o_ref[...] = x_ref[...] + g_ref[...] * y


def _res_matmul(st, a, w, x, mods, k_gate):
    m, k = a.shape
    n = w.shape[1]
    tm = _pick(st.length, (1024, 512, 256, 128))
    tn = _pick(n, (512, 256, 128))
    seg = st.seg(tm)
    return pl.pallas_call(
        _res_mm_kernel,
        grid=(n // tn, m // tm),
        in_specs=[
            pl.BlockSpec((tm, k), lambda j, i: (i, 0)),
            pl.BlockSpec((k, tn), lambda j, i: (0, j)),
            pl.BlockSpec((tm, tn), lambda j, i: (i, j)),
            pl.BlockSpec((None, None, 1, tn), lambda j, i: (k_gate, seg(i), 0, j)),
        ],
        out_specs=pl.BlockSpec((tm, tn), lambda j, i: (i, j)),
        out_shape=jax.ShapeDtypeStruct((m, n), F32),
        compiler_params=_cparams("parallel", "parallel"),
        name="res_matmul",
    )(a, w, x, mods)


def _merge_kernel(oa_ref, ob_ref, oc_ref, wa_ref, wb_ref, wc_ref, ga_ref, gb_ref, gc_ref, o_ref):
    y = ga_ref[...].astype(F32) * jnp.dot(oa_ref[...], wa_ref[...], preferred_element_type=F32)
    y += gb_ref[...].astype(F32) * jnp.dot(ob_ref[...], wb_ref[...], preferred_element_type=F32)
    y += gc_ref[...].astype(F32) * jnp.dot(oc_ref[...], wc_ref[...], preferred_element_type=F32)
    o_ref[...] = y.astype(o_ref.dtype)


def _merge(st, oa, ob, oc, w_pa, w_pb, w_pc, gates):
    m = oa.shape[0]
    d = w_pa.shape[1]
    tm = _pick(st.length, (512, 256, 128))
    tn = _pick(d, (512, 256, 128))
    nj = d // tn
    a_spec = lambda kk: pl.BlockSpec((tm, kk), lambda j, i: (i, 0))
    w_spec = lambda kk: pl.BlockSpec((kk, tn), lambda j, i: (0, j))
    g_spec = lambda br: pl.BlockSpec((tm, tn), lambda j, i: (i, br * nj + j))
    return pl.pallas_call(
        _merge_kernel,
        grid=(nj, m // tm),
        in_specs=[a_spec(A_V), a_spec(B_V), a_spec(C_W), w_spec(A_V), w_spec(B_V), w_spec(C_W),
                  g_spec(0), g_spec(1), g_spec(2)],
        out_specs=pl.BlockSpec((tm, tn), lambda j, i: (i, j)),
        out_shape=jax.ShapeDtypeStruct((m, d), BF16),
        compiler_params=_cparams("parallel", "parallel"),
        name="merge",
    )(oa, ob, oc, w_pa, w_pb, w_pc, gates, gates, gates)


def _attn_kernel(*refs, tk, n_lat):
    if n_lat:
        scal_ref, q_ref, kc_ref, vc_ref, kl_ref, vl_ref, sub_ref, o_ref, m_ref, l_ref, acc_ref = refs
    else:
        scal_ref, q_ref, kc_ref, vc_ref, sub_ref, o_ref, m_ref, l_ref, acc_ref = refs
    q = q_ref[...]
    m_ref[...] = jnp.full(m_ref.shape, -1e30, F32)
    l_ref[...] = jnp.zeros(l_ref.shape, F32)
    acc_ref[...] = jnp.zeros(acc_ref.shape, F32)

    def update(k, v):
        for mp in range(2):
            s = lax.dot_general(q[:, mp * A_DIM:(mp + 1) * A_DIM], k[:, mp * A_DIM:(mp + 1) * A_DIM],
                                (((1,), (1,)), ((), ())), preferred_element_type=F32)
            m_prev = m_ref[mp]
            m_new = jnp.maximum(m_prev, jnp.max(s, axis=-1, keepdims=True))
            alpha = jnp.exp(m_prev - m_new)
            p = jnp.exp(s - m_new)
            l_ref[mp] = alpha * l_ref[mp] + jnp.sum(p, axis=-1, keepdims=True)
            acc_ref[mp] = alpha * acc_ref[mp] + jnp.dot(p.astype(BF16), v, preferred_element_type=F32)
            m_ref[mp] = m_new

    update(kc_ref[...], vc_ref[...])
    if n_lat:
        def body(c, carry):
            r = pl.multiple_of(c * tk, tk)
            update(kl_ref[pl.ds(r, tk), :], vl_ref[pl.ds(r, tk), :])
            return carry

        lax.fori_loop(0, n_lat, body, 0)

    lam = scal_ref[0]
    post = scal_ref[1]
    o = acc_ref[0] / l_ref[0] - lam * (acc_ref[1] / l_ref[1])
    y = o * lax.rsqrt(jnp.mean(o * o, axis=-1, keepdims=True) + NORM_EPS) * sub_ref[...] * post
    o_ref[...] = y.astype(o_ref.dtype)


def _attention(st_q, p_q, p_ctx, p_lat, scal, subln, lc, ll):
    batch = st_q.batch
    hw = 2 * A_DIM
    tq = _pick(st_q.length, (1024, 512, 256, 128))
    nq = st_q.length // tq
    with_lat = not st_q.is_ctx
    tk = _pick(ll, (512, 256, 128))
    n_lat = ll // tk if with_lat else 0
    in_specs = [
        _smem(),
        pl.BlockSpec((tq, hw), lambda b, h, i: (b * nq + i, OFF_QA // hw + h)),
        pl.BlockSpec((lc, hw), lambda b, h, i: (b, OFF_KA // hw + h)),
        pl.BlockSpec((lc, hw), lambda b, h, i: (b, OFF_VA // hw + h)),
    ]
    args = [scal, p_q, p_ctx, p_ctx]
    if with_lat:
        in_specs += [
            pl.BlockSpec((ll, hw), lambda b, h, i: (b, OFF_KA // hw + h)),
            pl.BlockSpec((ll, hw), lambda b, h, i: (b, OFF_VA // hw + h)),
        ]
        args += [p_lat, p_lat]
    in_specs.append(pl.BlockSpec((1, hw), lambda b, h, i: (0, 0)))
    args.append(subln.reshape(1, hw))
    return pl.pallas_call(
        functools.partial(_attn_kernel, tk=tk, n_lat=n_lat),
        grid=(batch, A_HEADS, nq),
        in_specs=in_specs,
        out_specs=pl.BlockSpec((tq, hw), lambda b, h, i: (b * nq + i, h)),
        out_shape=jax.ShapeDtypeStruct((st_q.rows, A_V), BF16),
        scratch_shapes=[pltpu.VMEM((2, tq, 1), F32), pltpu.VMEM((2, tq, 1), F32),
                        pltpu.VMEM((2, tq, hw), F32)],
        compiler_params=_cparams("parallel", "parallel", "parallel"),
        name="diff_attn",
    )(*args)


def _ret_kernel(*refs, bwd, tb, has_init, finalize, want_state):
    refs = list(refs)
    lg_ref, q_ref, k_ref, v_ref = refs[:4]
    pos_ = 4
    s0_ref = None
    if has_init:
        s0_ref = refs[pos_]
        pos_ += 1
    if finalize:
        oprev_ref, gate_ref, sub_ref = refs[pos_:pos_ + 3]
        pos_ += 3
    o_ref = refs[pos_]
    pos_ += 1
    sfin_ref = None
    if want_state:
        sfin_ref = refs[pos_]
        pos_ += 1
    s_ref = refs[pos_]

    h = pl.program_id(1)
    t = pl.program_id(2)
    lg = lg_ref[h]

    @pl.when(t == 0)
    def _():
        if has_init:
            s_ref[...] = s0_ref[...]
        else:
            s_ref[...] = jnp.zeros(s_ref.shape, F32)

    q = q_ref[...]
    k = k_ref[...]
    v = v_ref[...]
    ii = lax.broadcasted_iota(jnp.int32, (tb, tb), 0)
    jj = lax.broadcasted_iota(jnp.int32, (tb, tb), 1)
    rel = ((jj - ii) if bwd else (ii - jj)).astype(F32)
    decay = jnp.where(rel >= 0, jnp.exp(lg * jnp.maximum(rel, 0.0)), 0.0)
    scores = lax.dot_general(q, k, (((1,), (1,)), ((), ())), preferred_element_type=F32) * decay
    intra = jnp.dot(scores.astype(BF16), v, preferred_element_type=F32)
    pos = lax.broadcasted_iota(jnp.int32, (tb, 1), 0).astype(F32)
    q_exp = (tb - pos) if bwd else (pos + 1.0)
    k_exp = pos if bwd else (tb - 1.0 - pos)
    state = s_ref[...]
    q_dec = (q.astype(F32) * jnp.exp(lg * q_exp)).astype(BF16)
    inter = jnp.dot(q_dec, state.astype(BF16), preferred_element_type=F32)
    out = intra + inter
    k_dec = (k.astype(F32) * jnp.exp(lg * k_exp)).astype(BF16)
    kv = lax.dot_general(k_dec, v, (((0,), (0,)), ((), ())), preferred_element_type=F32)
    new_state = jnp.exp(jnp.full((1, 1), lg * tb, F32)) * state + kv
    s_ref[...] = new_state

    if finalize:
        tot = out + oprev_ref[...]
        y = tot * lax.rsqrt(jnp.mean(tot * tot, axis=-1, keepdims=True) + NORM_EPS) * sub_ref[...]
        o_ref[...] = (y * gate_ref[...].astype(F32)).astype(o_ref.dtype)
    else:
        o_ref[...] = out

    if want_state:
        @pl.when(t == pl.num_programs(2) - 1)
        def _():
            sfin_ref[...] = new_state


def _retention_dir(st, p, log_g, bwd, s0, oprev, subln, want_state):
    batch = st.batch
    tb = _pick(st.length, (512, 256, 128))
    nt = st.length // tb
    finalize = oprev is not None
    has_init = s0 is not None

    if bwd:
        rb = lambda b, t: b * nt + (nt - 1 - t)
    else:
        rb = lambda b, t: b * nt + t

    in_specs = [
        _smem(),
        pl.BlockSpec((tb, B_DK), lambda b, h, t: (rb(b, t), OFF_QB // B_DK + h)),
        pl.BlockSpec((tb, B_DK), lambda b, h, t: (rb(b, t), OFF_KB // B_DK + h)),
        pl.BlockSpec((tb, B_DV), lambda b, h, t: (rb(b, t), OFF_VB // B_DV + h)),
    ]
    args = [log_g, p, p, p]
    if has_init:
        in_specs.append(pl.BlockSpec((None, None, B_DK, B_DV), lambda b, h, t: (b, h, 0, 0)))
        args.append(s0)
    if finalize:
        in_specs += [
            pl.BlockSpec((tb, B_DV), lambda b, h, t: (rb(b, t), h)),
            pl.BlockSpec((tb, B_DV), lambda b, h, t: (rb(b, t), OFF_GB // B_DV + h)),
            pl.BlockSpec((1, B_DV), lambda b, h, t: (0, 0)),
        ]
        args += [oprev, p, subln.reshape(1, B_DV)]
    out_specs = [pl.BlockSpec((tb, B_DV), lambda b, h, t: (rb(b, t), h))]
    out_shape = [jax.ShapeDtypeStruct((st.rows, B_V), BF16 if finalize else F32)]
    if want_state:
        out_specs.append(pl.BlockSpec((None, None, B_DK, B_DV), lambda b, h, t: (b, h, 0, 0)))
        out_shape.append(jax.ShapeDtypeStruct((batch, B_HEADS, B_DK, B_DV), F32))
    res = pl.pallas_call(
        functools.partial(_ret_kernel, bwd=bwd, tb=tb, has_init=has_init, finalize=finalize,
                          want_state=want_state),
        grid=(batch, B_HEADS, nt),
        in_specs=in_specs,
        out_specs=out_specs,
        out_shape=out_shape,
        scratch_shapes=[pltpu.VMEM((B_DK, B_DV), F32)],
        compiler_params=_cparams("parallel", "parallel", "arbitrary"),
        name="retention_bwd" if bwd else "retention_fwd",
    )(*args)
    return (res[0], res[1]) if want_state else (res[0], None)


def _retention(st_c, st_l, p_ctx, p_lat, log_f, log_b, subln):
    of_c, s_f = _retention_dir(st_c, p_ctx, log_f, False, None, None, None, True)
    ob_c, s_b = _retention_dir(st_c, p_ctx, log_b, True, None, of_c, subln, True)
    of_l, _ = _retention_dir(st_l, p_lat, log_f, False, s_f, None, None, False)
    ob_l, _ = _retention_dir(st_l, p_lat, log_b, True, s_b, of_l, subln, False)
    return ob_c, ob_l


def _dft_cos_sin(n):
    idx = np.arange(n)
    ang = 2.0 * np.pi * ((idx[:, None] * idx[None, :]) % n) / n
    return np.cos(ang), np.sin(ang)


def _fft_stage1_kernel(x_ref, m1_ref, twc_ref, tws_ref, o_ref, *, n1, n2b, cw):
    res = jnp.dot(m1_ref[...], x_ref[...], preferred_element_type=F32)
    for s in range(n2b):
        ar = res[:n1, s * cw:(s + 1) * cw]
        ai = res[n1:, s * cw:(s + 1) * cw]
        c = twc_ref[s]
        sn = tws_ref[s]
        o_ref[0, :, s * cw:(s + 1) * cw] = (ar * c + ai * sn).astype(o_ref.dtype)
        o_ref[1, :, s * cw:(s + 1) * cw] = (ai * c - ar * sn).astype(o_ref.dtype)


def _fft_final_kernel(z_ref, m2_ref, cc_ref, sc_ref, o_ref, *, k1b, n2, cw, complex_in):
    for kk in range(k1b):
        if complex_in:
            z = z_ref[:, kk].reshape(2 * n2, cw)
        else:
            z = z_ref[...]
        x = jnp.dot(m2_ref[...], z, preferred_element_type=F32)
        xr = x[:n2].astype(BF16)
        xi = x[n2:].astype(BF16)
        for g in range(C_GROUPS):
            lo = g * C_GROUP_DIM
            y = jnp.dot(xr[:, lo:lo + C_GROUP_DIM], cc_ref[...], preferred_element_type=F32)
            y += jnp.dot(xi[:, lo:lo + C_GROUP_DIM], sc_ref[...], preferred_element_type=F32)
            o_ref[:, kk * cw + lo:kk * cw + lo + C_GROUP_DIM] = y.astype(o_ref.dtype)


def _fourier(st, u):
    batch, length = st.batch, st.length
    cw = C_W
    n2 = 128 if (length % 128 == 0 and length > 256) else length
    n1 = length // n2
    cc, sc = _dft_cos_sin(C_GROUP_DIM)
    cc = jnp.asarray(cc / math.sqrt(C_GROUP_DIM), BF16)
    sc = jnp.asarray(sc / math.sqrt(C_GROUP_DIM), BF16)
    c2, s2 = _dft_cos_sin(n2)
    if n1 == 1:
        m2 = jnp.asarray(np.concatenate([c2, -s2], axis=0) / math.sqrt(n2), BF16)
        z = u.reshape(batch, n2, cw)
        z_spec = pl.BlockSpec((None, n2, cw), lambda b, i: (b, 0, 0))
        k1b = 1
    else:
        c1, s1 = _dft_cos_sin(n1)
        m1 = jnp.asarray(np.concatenate([c1, -s1], axis=0) / math.sqrt(n1), BF16)
        i2 = np.arange(n2)[:, None]
        k1 = np.arange(n1)[None, :]
        ang = 2.0 * np.pi * ((i2 * k1) % length) / length
        twc = jnp.asarray(np.cos(ang)[:, :, None], F32)
        tws = jnp.asarray(np.sin(ang)[:, :, None], F32)
        n2b = _pick(n2, (4, 2, 1))
        stage1 = pl.pallas_call(
            functools.partial(_fft_stage1_kernel, n1=n1, n2b=n2b, cw=cw),
            grid=(batch, n2 // n2b),
            in_specs=[
                pl.BlockSpec((None, n1, n2b * cw), lambda b, j: (b, 0, j)),
                pl.BlockSpec((2 * n1, n1), lambda b, j: (0, 0)),
                pl.BlockSpec((n2b, n1, 1), lambda b, j: (j, 0, 0)),
                pl.BlockSpec((n2b, n1, 1), lambda b, j: (j, 0, 0)),
            ],
            out_specs=pl.BlockSpec((None, 2, n1, n2b * cw), lambda b, j: (b, 0, 0, j)),
            out_shape=jax.ShapeDtypeStruct((batch, 2, n1, n2 * cw), BF16),
            compiler_params=_cparams("parallel", "parallel"),
            name="fft_stage1",
        )(u.reshape(batch, n1, n2 * cw), m1, twc, tws)
        m2 = jnp.asarray(np.block([[c2, s2], [-s2, c2]]) / math.sqrt(n2), BF16)
        z = stage1.reshape(batch, 2, n1, n2, cw)
        k1b = _pick(n1, (4, 2, 1))
        z_spec = pl.BlockSpec((None, 2, k1b, n2, cw), lambda b, i: (b, 0, i, 0, 0))
    out = pl.pallas_call(
        functools.partial(_fft_final_kernel, k1b=k1b, n2=n2, cw=cw, complex_in=n1 > 1),
        grid=(batch, n1 // k1b),
        in_specs=[
            z_spec,
            pl.BlockSpec(m2.shape, lambda b, i: (0, 0)),
            pl.BlockSpec(cc.shape, lambda b, i: (0, 0)),
            pl.BlockSpec(sc.shape, lambda b, i: (0, 0)),
        ],
        out_specs=pl.BlockSpec((None, n2, k1b * cw), lambda b, i: (b, 0, i)),
        out_shape=jax.ShapeDtypeStruct((batch, n2, n1 * cw), BF16),
        compiler_params=_cparams("parallel", "parallel"),
        name="fft_final",
    )(z, m2, cc, sc)
    return out.reshape(batch * length, cw)


def _convglu_kernel(g_ref, v_ref, gp_ref, gn_ref, vp_ref, vn_ref, wg_ref, wv_ref, bg_ref, bv_ref, o_ref,
                    *, tr, seq_tiles):
    i = pl.program_id(1)
    first = (i % seq_tiles) == 0
    last = (i % seq_tiles) == seq_tiles - 1
    row = lax.broadcasted_iota(jnp.int32, (tr, 1), 0)

    def conv(u_ref, p_ref, n_ref, w_ref, b_ref):
        u = u_ref[...].astype(F32)
        hb = p_ref.shape[0]
        prev_row = jnp.where(first, 0.0, p_ref[hb - 1:hb, :].astype(F32))
        next_row = jnp.where(last, 0.0, n_ref[0:1, :].astype(F32))
        up = jnp.where(row == 0, prev_row, pltpu.roll(u, 1, 0))
        dn = jnp.where(row == tr - 1, next_row, pltpu.roll(u, tr - 1, 0))
        return up * w_ref[0:1, :] + u * w_ref[1:2, :] + dn * w_ref[2:3, :] + b_ref[...]

    a = conv(g_ref, gp_ref, gn_ref, wg_ref, bg_ref)
    b = conv(v_ref, vp_ref, vn_ref, wv_ref, bv_ref)
    o_ref[...] = (a * jax.nn.sigmoid(a) * b).astype(o_ref.dtype)


def _convglu(st, u, conv_w, conv_b):
    m, f2 = u.shape
    f = f2 // 2
    tr = _pick(st.length, (512, 256, 128))
    tc = _pick(f, (1024, 512, 256, 128))
    nj = f // tc
    hb = V7X_BF16_SUBLANES
    nhb = m // hb
    rpt = tr // hb
    prev = lambda i: jnp.maximum(i * rpt - 1, 0)
    nxt = lambda i: jnp.minimum((i + 1) * rpt, nhb - 1)
    return pl.pallas_call(
        functools.partial(_convglu_kernel, tr=tr, seq_tiles=st.length // tr),
        grid=(nj, m // tr),
        in_specs=[
            pl.BlockSpec((tr, tc), lambda j, i: (i, j)),
            pl.BlockSpec((tr, tc), lambda j, i: (i, nj + j)),
            pl.BlockSpec((hb, tc), lambda j, i: (prev(i), j)),
            pl.BlockSpec((hb, tc), lambda j, i: (nxt(i), j)),
            pl.BlockSpec((hb, tc), lambda j, i: (prev(i), nj + j)),
            pl.BlockSpec((hb, tc), lambda j, i: (nxt(i), nj + j)),
            pl.BlockSpec((3, tc), lambda j, i: (0, j)),
            pl.BlockSpec((3, tc), lambda j, i: (0, nj + j)),
            pl.BlockSpec((1, tc), lambda j, i: (0, j)),
            pl.BlockSpec((1, tc), lambda j, i: (0, nj + j)),
        ],
        out_specs=pl.BlockSpec((tr, tc), lambda j, i: (i, j)),
        out_shape=jax.ShapeDtypeStruct((m, f), BF16),
        compiler_params=_cparams("parallel", "parallel"),
        name="conv_glu",
    )(u, u, u, u, u, u, conv_w, conv_w, conv_b.reshape(1, f2), conv_b.reshape(1, f2))


def _rope_tables(n_tokens):
    rows = n_tokens // GRID_W
    r, col = jnp.meshgrid(jnp.arange(rows), jnp.arange(GRID_W), indexing="ij")
    pos = jnp.stack([r.reshape(-1), col.reshape(-1)], axis=-1).astype(F32)
    n_freq = A_DIM // 4
    inv_freq = ROPE_BASE ** (-jnp.arange(n_freq, dtype=F32) / n_freq)
    ang = pos[:, :, None, None] * inv_freq
    ang = jnp.broadcast_to(ang, (n_tokens, 2, 2, n_freq)).reshape(n_tokens, A_DIM)
    sign = jnp.where((jnp.arange(A_DIM) % (A_DIM // 2)) < (A_DIM // 4), -1.0, 1.0).astype(F32)
    return jnp.cos(ang), jnp.sin(ang) * sign


def _mixer_and_ffn(st_l, st_c, cos, sin_s, x_l, x_c, wts):
    mods = wts["mods"]
    w_in = wts["w_in"].astype(BF16)
    d = x_l.shape[1]
    streams = ((st_l, x_l), (st_c, x_c))

    proj = []
    for st, x in streams:
        h = _norm_mod(st, x, wts["norm1_g"], mods, 0)
        tabs = (cos, sin_s) if not st.is_ctx else (cos[:st.length], sin_s[:st.length])
        p_mix = _proj(st, h, w_in, 0, MIX_COLS, *tabs)
        p_u = _proj(st, h, w_in, OFF_UC, C_W, *tabs)
        p_g = _proj(st, h, w_in, OFF_GATES, 3 * d, *tabs)
        proj.append((p_mix, p_u, p_g))
    (pm_l, pu_l, pg_l), (pm_c, pu_c, pg_c) = proj

    lc, ll = st_c.length, st_l.length
    oa_l = _attention(st_l, pm_l, pm_c, pm_l, wts["scal"], wts["attn_subln"], lc, ll)
    oa_c = _attention(st_c, pm_c, pm_c, None, wts["scal"], wts["attn_subln"], lc, ll)
    ob_c, ob_l = _retention(st_c, st_l, pm_c, pm_l, wts["log_f"], wts["log_b"], wts["ret_subln"])
    oc_l = _fourier(st_l, pu_l)
    oc_c = _fourier(st_c, pu_c)

    w_pa = wts["w_pa"].astype(BF16)
    w_pb = wts["w_pb"].astype(BF16)
    w_pc = wts["w_pc"].astype(BF16)
    w_out = wts["w_out"].astype(BF16)
    w_up = wts["w_up"].astype(BF16)
    w_down = wts["w_down"].astype(BF16)

    outs = []
    for st, x, oa, ob, oc, pg in ((st_l, x_l, oa_l, ob_l, oc_l, pg_l), (st_c, x_c, oa_c, ob_c, oc_c, pg_c)):
        y = _merge(st, oa, ob, oc, w_pa, w_pb, w_pc, pg)
        x1 = _res_matmul(st, y, w_out, x, mods, 2)
        h2 = _norm_mod(st, x1, wts["norm2_g"], mods, 3)
        u = _matmul(st, h2, w_up, BF16)
        act = _convglu(st, u, wts["conv_w"], wts["conv_b"])
        outs.append(_res_matmul(st, act, w_down, x1, mods, 5))
    return outs[0], outs[1]


def kernel(x, c, ctx, c_ctx, w_ada_down, w_ada_up, b_ada, norm1_g, norm2_g, w_in, lam_q1, lam_k1, lam_q2,
           lam_k2, attn_subln, ret_decay_fwd, ret_decay_bwd, ret_subln, w_pa, w_pb, w_pc, w_out, w_up,
           conv_w, conv_b, w_down, final_g):
    batch, seq, d = x.shape
    lc = ctx.shape[1]
    depth = w_in.shape[0]
    assert batch + 1 <= V7X_MOD_ROWS
    st_l = _Stream(batch, seq, False)
    st_c = _Stream(batch, lc, True)

    cond = jnp.zeros((V7X_MOD_ROWS, d), F32).at[:batch].set(c.astype(F32)).at[batch].set(c_ctx.astype(F32))
    mods = _ada_params(cond, w_ada_down, w_ada_up, b_ada)

    lam_init = jnp.asarray([0.8 - 0.6 * math.exp(-0.3 * l) for l in range(depth)], F32)
    lam = (jnp.exp(jnp.sum(lam_q1.astype(F32) * lam_k1.astype(F32), axis=-1))
           - jnp.exp(jnp.sum(lam_q2.astype(F32) * lam_k2.astype(F32), axis=-1)) + lam_init)
    scal = jnp.stack([lam, 1.0 - lam_init], axis=-1)
    log_f = jax.nn.log_sigmoid(ret_decay_fwd.astype(F32))
    log_b = jax.nn.log_sigmoid(ret_decay_bwd.astype(F32))
    cos, sin_s = _rope_tables(seq)

    xs = dict(mods=mods, scal=scal, log_f=log_f, log_b=log_b, norm1_g=norm1_g, norm2_g=norm2_g, w_in=w_in,
              attn_subln=attn_subln, ret_subln=ret_subln, w_pa=w_pa, w_pb=w_pb, w_pc=w_pc, w_out=w_out,
              w_up=w_up, conv_w=conv_w, conv_b=conv_b, w_down=w_down)

    def layer(carry, wts):
        x_l, x_c = carry
        return _mixer_and_ffn(st_l, st_c, cos, sin_s, x_l, x_c, wts), None

    (x_l, _), _ = lax.scan(layer, (x.reshape(batch * seq, d), ctx.reshape(batch * lc, d)), xs)
    return _final_norm(x_l, final_g).reshape(batch, seq, d)
```

```python
import functools
import math

import numpy as np
import jax
import jax.numpy as jnp
from jax import lax
from jax.experimental import pallas as pl
from jax.experimental.pallas import tpu as pltpu

F32 = jnp.float32
BF16 = jnp.bfloat16

GRID_W = 64
N_MOD = 6
A_HEADS = 8
A_DIM = 128
B_HEADS = 8
B_DK = 128
B_DV = 256
C_GROUPS = 4
C_GROUP_DIM = 512
ROPE_BASE = 10000.0
NORM_EPS = 1e-6
LOG2_E = math.log2(math.e)
A_QK =A_HEADS * 2 * A_DIM
A_V = A_HEADS * 2 * A_DIM
B_QK = B_HEADS * B_DK
B_V = B_HEADS * B_DV
C_W = C_GROUPS * C_GROUP_DIM
OFF_KA = 0
OFF_VA = OFF_KA + A_QK
OFF_KB = OFF_VA + A_V
OFF_VB = OFF_KB + B_QK
OFF_QA = OFF_VB + B_V
OFF_QB = OFF_QA + A_QK
OFF_GB = OFF_QB + B_QK
OFF_UC = OFF_GB + B_V
OFF_GATES = OFF_UC + C_W
MIX_COLS = OFF_UC

V7X_LANES = 128
V7X_BF16_SUBLANES = 16
V7X_MOD_ROWS = 8
V7X_VMEM_LIMIT_BYTES = 56 * 1024 * 1024


def _cparams(*sem):
    return pltpu.CompilerParams(dimension_semantics=sem, vmem_limit_bytes=V7X_VMEM_LIMIT_BYTES)


def _pick(n, prefs):
    for p in prefs:
        if n % p == 0:
            return p
    return n


def _smem():
    return pl.BlockSpec(memory_space=pltpu.SMEM)


def _ada_kernel(cond_ref, wd_ref, wu_ref, b_ref, o_ref, t_ref):
    @pl.when((pl.program_id(1) == 0) & (pl.program_id(2) == 0))
    def _():
        cnd = cond_ref[...]
        s = cnd * jax.nn.sigmoid(cnd)
        t_ref[...] = jnp.dot(s.astype(BF16), wd_ref[...].astype(BF16), preferred_element_type=F32)

    o_ref[...] = jnp.dot(t_ref[...].astype(BF16), wu_ref[...].astype(BF16),
                         preferred_element_type=F32) + b_ref[...]


def _ada_params(cond, w_down, w_up, b):
    depth, d, r = w_down.shape
    tn = _pick(d, (2048, 1024, 512, 256, 128))
    nj = d // tn
    out = pl.pallas_call(
        _ada_kernel,
        grid=(depth, N_MOD, nj),
        in_specs=[
            pl.BlockSpec((V7X_MOD_ROWS, d), lambda l, k, j: (0, 0)),
            pl.BlockSpec((None, d, r), lambda l, k, j: (l, 0, 0)),
            pl.BlockSpec((None, r, tn), lambda l, k, j: (l, 0, k * nj + j)),
            pl.BlockSpec((None, 1, tn), lambda l, k, j: (l, 0, k * nj + j)),
        ],
        out_specs=pl.BlockSpec((None, None, V7X_MOD_ROWS, tn), lambda l, k, j: (l, k, 0, j)),
        out_shape=jax.ShapeDtypeStruct((depth, N_MOD, V7X_MOD_ROWS, d), F32),
        scratch_shapes=[pltpu.VMEM((V7X_MOD_ROWS, r), F32)],
        compiler_params=_cparams("arbitrary", "arbitrary", "arbitrary"),
        name="ada_params",
    )(cond, w_down, w_up, b.reshape(depth, 1, N_MOD * d))
    return out.reshape(depth, N_MOD, V7X_MOD_ROWS, 1, d)


class _Stream:
    def __init__(self, batch, length, is_ctx):
        self.batch = batch
        self.length = length
        self.is_ctx = is_ctx
        self.rows = batch * length

    def seg(self, tile_rows):
        if self.is_ctx:
            return lambda i: self.batch
        per = self.length // tile_rows
        return lambda i: i // per


def _norm_mod_kernel(x_ref, g_ref, sh_ref, sc_ref, o_ref):
    x = x_ref[...]
    y = x * lax.rsqrt(jnp.mean(x * x, axis=-1, keepdims=True) + NORM_EPS) * g_ref[...]
    o_ref[...] = (y * (1.0 + sc_ref[...]) + sh_ref[...]).astype(o_ref.dtype)


def _norm_kernel(x_ref, g_ref, o_ref):
    x = x_ref[...]
    y = x * lax.rsqrt(jnp.mean(x * x, axis=-1, keepdims=True) + NORM_EPS) * g_ref[...]
    o_ref[...] = y.astype(o_ref.dtype)


def _norm_mod(st, x, g, mods, k_shift):
    m, d = x.shape
    tr = _pick(st.length, (256, 128, 64, 32, 16))
    seg = st.seg(tr)
    return pl.pallas_call(
        _norm_mod_kernel,
        grid=(m // tr,),
        in_specs=[
            pl.BlockSpec((tr, d), lambda i: (i, 0)),
            pl.BlockSpec((1, d), lambda i: (0, 0)),
            pl.BlockSpec((None, None, 1, d), lambda i: (k_shift, seg(i), 0, 0)),
            pl.BlockSpec((None, None, 1, d), lambda i: (k_shift + 1, seg(i), 0, 0)),
        ],
        out_specs=pl.BlockSpec((tr, d), lambda i: (i, 0)),
        out_shape=jax.ShapeDtypeStruct((m, d), BF16),
        compiler_params=_cparams("parallel"),
        name="norm_mod",
    )(x, g.reshape(1, d), mods, mods)


def _final_norm(x, g):
    m, d = x.shape
    tr = _pick(m, (256, 128, 64, 32, 16, 8))
    return pl.pallas_call(
        _norm_kernel,
        grid=(m // tr,),
        in_specs=[pl.BlockSpec((tr, d), lambda i: (i, 0)), pl.BlockSpec((1, d), lambda i: (0, 0))],
        out_specs=pl.BlockSpec((tr, d), lambda i: (i, 0)),
        out_shape=jax.ShapeDtypeStruct((m, d), F32),
        compiler_params=_cparams("parallel"),
        name="final_norm",
    )(x, g.reshape(1, d))


def _in_range(c0, off, width):
    return (c0 >= off) & (c0 < off + width)


def _proj_kernel(h_ref, w_ref, cos_ref, sin_ref, o_ref, *, tn, col0, use_rope):
    acc = jnp.dot(h_ref[...], w_ref[...], preferred_element_type=F32)
    c0 = col0 + pl.program_id(0) * tn
    in_kb = _in_range(c0, OFF_KB, B_QK)
    in_qa = _in_range(c0, OFF_QA, A_QK)
    is_rope = _in_range(c0, OFF_KA, A_QK) | in_kb | in_qa | _in_range(c0, OFF_QB, B_QK)
    is_silu = _in_range(c0, OFF_GB, B_V)
    is_sig = c0 >= OFF_GATES
    is_plain = jnp.logical_not(is_rope | is_silu | is_sig)

    @pl.when(is_rope)
    def _():
        scale = jnp.where(in_kb, B_DK ** -0.5, jnp.where(in_qa, A_DIM ** -0.5 * LOG2_E, 1.0)).astype(F32)
        if use_rope:
            cos = cos_ref[...] * scale
            sin = sin_ref[...] * scale
            lane = lax.broadcasted_iota(jnp.int32, cos.shape, 1)
            first_half = (lane % (A_DIM // 2)) < (A_DIM // 4)
            for c in range(tn // A_DIM):
                xc = acc[:, c * A_DIM:(c + 1) * A_DIM]
                rot = jnp.where(first_half, pltpu.roll(xc, A_DIM - A_DIM // 4, 1),
                                pltpu.roll(xc, A_DIM // 4, 1))
                o_ref[:, c * A_DIM:(c + 1) * A_DIM] = (xc * cos + rot * sin).astype(o_ref.dtype)
        else:
            o_ref[...] = (acc * scale).astype(o_ref.dtype)

    @pl.when(is_silu)
    def _():
        o_ref[...] = (acc * jax.nn.sigmoid(acc)).astype(o_ref.dtype)

    @pl.when(is_sig)
    def _():
        o_ref[...] = jax.nn.sigmoid(acc).astype(o_ref.dtype)

    @pl.when(is_plain)
    def _():
        o_ref[...] = acc.astype(o_ref.dtype)


def _proj(st, h, w, col0, ncols, cos, sin_signed):
    m, k = h.shape
    tm = _pick(st.length, (1024, 512, 256, 128))
    tn = _pick(ncols, (1024, 512, 256, 128))
    per = st.length // tm
    use_rope = not st.is_ctx
    j0 = col0 // tn
    return pl.pallas_call(
        functools.partial(_proj_kernel, tn=tn, col0=col0, use_rope=use_rope),
        grid=(ncols // tn, m // tm),
        in_specs=[
            pl.BlockSpec((tm, k), lambda j, i: (i, 0)),
            pl.BlockSpec((k, tn), lambda j, i: (0, j0 + j)),
            pl.BlockSpec((tm, A_DIM), lambda j, i: (i % per, 0)),
            pl.BlockSpec((tm, A_DIM), lambda j, i: (i % per, 0)),
        ],
        out_specs=pl.BlockSpec((tm, tn), lambda j, i: (i, j)),
        out_shape=jax.ShapeDtypeStruct((m, ncols), BF16),
        compiler_params=_cparams("parallel", "parallel"),
        name="in_proj",
    )(h, w, cos, sin_signed)


def _mm_kernel(a_ref, w_ref, o_ref):
    o_ref[...] = jnp.dot(a_ref[...], w_ref[...], preferred_element_type=F32).astype(o_ref.dtype)


def _matmul(st, a, w, out_dtype):
    m, k = a.shape
    n = w.shape[1]
    tm = _pick(st.length, (1024, 512, 256, 128))
    tn = _pick(n, (1024, 512, 256, 128))
    return pl.pallas_call(
        _mm_kernel,
        grid=(n // tn, m // tm),
        in_specs=[pl.BlockSpec((tm, k), lambda j, i: (i, 0)), pl.BlockSpec((k, tn), lambda j, i: (0, j))],
        out_specs=pl.BlockSpec((tm, tn), lambda j, i: (i, j)),
        out_shape=jax.ShapeDtypeStruct((m, n), out_dtype),
        compiler_params=_cparams("parallel", "parallel"),
        name="ffn_up",
    )(a, w)


def _res_mm_kernel(a_ref, w_ref, x_ref, g_ref, o_ref):
    y = jnp.dot(a_ref[...], w_ref[...], preferred_element_type=F32)
    o_ref[...] = x_ref[...] + g_ref[...] * y


def _res_matmul(st, a, w, x, mods, k_gate):
    m, k = a.shape
    n = w.shape[1]
    tm = _pick(st.length, (1024, 512, 256, 128))
    tn = _pick(n, (512, 256, 128))
    seg = st.seg(tm)
    return pl.pallas_call(
        _res_mm_kernel,
        grid=(n // tn, m // tm),
        in_specs=[
            pl.BlockSpec((tm, k), lambda j, i: (i, 0)),
            pl.BlockSpec((k, tn), lambda j, i: (0, j)),
            pl.BlockSpec((tm, tn), lambda j, i: (i, j)),
            pl.BlockSpec((None, None, 1, tn), lambda j, i: (k_gate, seg(i), 0, j)),
        ],
        out_specs=pl.BlockSpec((tm, tn), lambda j, i: (i, j)),
        out_shape=jax.ShapeDtypeStruct((m, n), F32),
        compiler_params=_cparams("parallel", "parallel"),
        name="res_matmul",
    )(a, w, x, mods)


def _merge_kernel(oa_ref, ob_ref, oc_ref, wa_ref, wb_ref, wc_ref, ga_ref, gb_ref, gc_ref, o_ref):
    y = ga_ref[...].astype(F32) * jnp.dot(oa_ref[...], wa_ref[...], preferred_element_type=F32)
    y += gb_ref[...].astype(F32) * jnp.dot(ob_ref[...], wb_ref[...], preferred_element_type=F32)
    y += gc_ref[...].astype(F32) * jnp.dot(oc_ref[...], wc_ref[...], preferred_element_type=F32)
    o_ref[...] = y.astype(o_ref.dtype)


def _merge(st, oa, ob, oc, w_pa, w_pb, w_pc, gates):
    m = oa.shape[0]
    d = w_pa.shape[1]
    tm = _pick(st.length, (512, 256, 128))
    tn = _pick(d, (512, 256, 128))
    nj = d // tn
    a_spec = lambda kk: pl.BlockSpec((tm, kk), lambda j, i: (i, 0))
    w_spec = lambda kk: pl.BlockSpec((kk, tn), lambda j, i: (0, j))
    g_spec = lambda br: pl.BlockSpec((tm, tn), lambda j, i: (i, br * nj + j))
    return pl.pallas_call(
        _merge_kernel,
        grid=(nj, m // tm),
        in_specs=[a_spec(A_V), a_spec(B_V), a_spec(C_W), w_spec(A_V), w_spec(B_V), w_spec(C_W),
                  g_spec(0), g_spec(1), g_spec(2)],
        out_specs=pl.BlockSpec((tm, tn), lambda j, i: (i, j)),
        out_shape=jax.ShapeDtypeStruct((m, d), BF16),
        compiler_params=_cparams("parallel", "parallel"),
        name="merge",
    )(oa, ob, oc, w_pa, w_pb, w_pc, gates, gates, gates)


def _attn_kernel(*refs, tq, tk, n_lat):
    if n_lat:
        scal_ref, q_ref, kc_ref, vc_ref, kl_ref, vl_ref, sub_ref, o_ref, m_ref, l_ref, acc_ref = refs
    else:
        scal_ref, q_ref, kc_ref, vc_ref, sub_ref, o_ref, m_ref, l_ref, acc_ref = refs
    q = q_ref[...]
    m_ref[...] = jnp.full(m_ref.shape, -1e30, F32)
    l_ref[...] = jnp.zeros(l_ref.shape, F32)
    acc_ref[...] = jnp.zeros(acc_ref.shape, F32)
    nt_dims = (((1,), (1,)), ((), ()))

    def update(k, v):
        s = jnp.concatenate(
            [lax.dot_general(q[:, :A_DIM], k[:, :A_DIM], nt_dims, preferred_element_type=F32),
             lax.dot_general(q[:, A_DIM:], k[:, A_DIM:], nt_dims, preferred_element_type=F32)], axis=0)
        m_prev = m_ref[...]
        m_next = jnp.maximum(m_prev, jnp.max(s, axis=1, keepdims=True))
        alpha = jnp.exp2(m_prev - m_next)
        tiles = [jnp.exp2(s[:, t * V7X_LANES:(t + 1) * V7X_LANES] - m_next)
                 for t in range(s.shape[1] // V7X_LANES)]
        l_ref[...] = alpha * l_ref[...] + functools.reduce(lambda a, b: a + b, tiles)
        p = jnp.concatenate(tiles, axis=1).astype(BF16)
        acc_ref[...] = (jnp.concatenate([alpha, alpha], axis=1) * acc_ref[...]
                        + jnp.dot(p, v, preferred_element_type=F32))
        m_ref[...] = m_next

    update(kc_ref[...], vc_ref[...])
    if n_lat:
        def body(c, carry):
            r = pl.multiple_of(c * tk, tk)
            update(kl_ref[pl.ds(r, tk), :], vl_ref[pl.ds(r, tk), :])
            return carry

        lax.fori_loop(0, n_lat, body, 0, unroll=2)

    lam = scal_ref[0]
    post = scal_ref[1]
    inv_l = 1.0 / jnp.sum(l_ref[...], axis=1, keepdims=True)
    o = acc_ref[:tq] * inv_l[:tq] - lam * (acc_ref[tq:] * inv_l[tq:])
    y = o * lax.rsqrt(jnp.mean(o * o, axis=-1, keepdims=True) + NORM_EPS) * sub_ref[...] * post
    o_ref[...] = y.astype(o_ref.dtype)


def _attention(st_q, p_q, p_ctx, p_lat, scal, subln, lc, ll):
    batch = st_q.batch
    hw = 2 * A_DIM
    tq = _pick(st_q.length, (256, 128))
    nq = st_q.length // tq
    with_lat = not st_q.is_ctx
    tk = _pick(ll, (512, 256, 128))
    n_lat = ll // tk if with_lat else 0
    in_specs = [
        _smem(),
        pl.BlockSpec((tq, hw), lambda b, h, i: (b * nq + i, OFF_QA // hw + h)),
        pl.BlockSpec((lc, hw), lambda b, h, i: (b, OFF_KA // hw + h)),
        pl.BlockSpec((lc, hw), lambda b, h, i: (b, OFF_VA // hw + h)),
    ]
    args = [scal, p_q, p_ctx, p_ctx]
    if with_lat:
        in_specs += [
            pl.BlockSpec((ll, hw), lambda b, h, i: (b, OFF_KA // hw + h)),
            pl.BlockSpec((ll, hw), lambda b, h, i: (b, OFF_VA // hw + h)),
        ]
        args += [p_lat, p_lat]
    in_specs.append(pl.BlockSpec((1, hw), lambda b, h, i: (0, 0)))
    args.append(subln.reshape(1, hw))
    return pl.pallas_call(
        functools.partial(_attn_kernel, tq=tq, tk=tk, n_lat=n_lat),
        grid=(batch, A_HEADS, nq),
        in_specs=in_specs,
        out_specs=pl.BlockSpec((tq, hw), lambda b, h, i: (b * nq + i, h)),
        out_shape=jax.ShapeDtypeStruct((st_q.rows, A_V), BF16),
        scratch_shapes=[pltpu.VMEM((2 * tq, V7X_LANES), F32), pltpu.VMEM((2 * tq, V7X_LANES), F32),
                        pltpu.VMEM((2 * tq, hw), F32)],
        compiler_params=_cparams("parallel", "parallel", "parallel"),
        name="diff_attn",
    )(*args)


def _ret_kernel(*refs, bwd, tb, has_init, finalize, want_state):
    refs = list(refs)
    lg_ref, q_ref, k_ref, v_ref = refs[:4]
    pos_ = 4
    s0_ref = None
    if has_init:
        s0_ref = refs[pos_]
        pos_ += 1
    if finalize:
        oprev_ref, gate_ref, sub_ref = refs[pos_:pos_ + 3]
        pos_ += 3
    o_ref = refs[pos_]
    pos_ += 1
    sfin_ref = None
    if want_state:
        sfin_ref = refs[pos_]
        pos_ += 1
    s_ref = refs[pos_]

    h = pl.program_id(1)
    t = pl.program_id(2)
    lg = lg_ref[h]

    @pl.when(t == 0)
    def _():
        if has_init:
            s_ref[...] = s0_ref[...]
        else:
            s_ref[...] = jnp.zeros(s_ref.shape, F32)

    q = q_ref[...]
    k = k_ref[...]
    v = v_ref[...]
    ii = lax.broadcasted_iota(jnp.int32, (tb, tb), 0)
    jj = lax.broadcasted_iota(jnp.int32, (tb, tb), 1)
    rel = ((jj - ii) if bwd else (ii - jj)).astype(F32)
    decay = jnp.where(rel >= 0, jnp.exp(lg * jnp.maximum(rel, 0.0)), 0.0)
    scores = lax.dot_general(q, k, (((1,), (1,)), ((), ())), preferred_element_type=F32) * decay
    intra = jnp.dot(scores.astype(BF16), v, preferred_element_type=F32)
    pos = lax.broadcasted_iota(jnp.int32, (tb, 1), 0).astype(F32)
    q_exp = (tb - pos) if bwd else (pos + 1.0)
    k_exp = pos if bwd else (tb - 1.0 - pos)
    state = s_ref[...]
    q_dec = (q.astype(F32) * jnp.exp(lg * q_exp)).astype(BF16)
    inter = jnp.dot(q_dec, state.astype(BF16), preferred_element_type=F32)
    out = intra + inter
    k_dec = (k.astype(F32) * jnp.exp(lg * k_exp)).astype(BF16)
    kv = lax.dot_general(k_dec, v, (((0,), (0,)), ((), ())), preferred_element_type=F32)
    new_state = jnp.exp(jnp.full((1, 1), lg * tb, F32)) * state + kv
    s_ref[...] = new_state

    if finalize:
        tot = out + oprev_ref[...]
        y = tot * lax.rsqrt(jnp.mean(tot * tot, axis=-1, keepdims=True) + NORM_EPS) * sub_ref[...]
        o_ref[...] = (y * gate_ref[...].astype(F32)).astype(o_ref.dtype)
    else:
        o_ref[...] = out

    if want_state:
        @pl.when(t == pl.num_programs(2) - 1)
        def _():
            sfin_ref[...] = new_state


def _retention_dir(st, p, log_g, bwd, s0, oprev, subln, want_state):
    batch = st.batch
    tb = _pick(st.length, (512, 256, 128))
    nt = st.length // tb
    finalize = oprev is not None
    has_init = s0 is not None

    if bwd:
        rb = lambda b, t: b * nt + (nt - 1 - t)
    else:
        rb = lambda b, t: b * nt + t

    in_specs = [
        _smem(),
        pl.BlockSpec((tb, B_DK), lambda b, h, t: (rb(b, t), OFF_QB // B_DK + h)),
        pl.BlockSpec((tb, B_DK), lambda b, h, t: (rb(b, t), OFF_KB // B_DK + h)),
        pl.BlockSpec((tb, B_DV), lambda b, h, t: (rb(b, t), OFF_VB // B_DV + h)),
    ]
    args = [log_g, p, p, p]
    if has_init:
        in_specs.append(pl.BlockSpec((None, None, B_DK, B_DV), lambda b, h, t: (b, h, 0, 0)))
        args.append(s0)
    if finalize:
        in_specs += [
            pl.BlockSpec((tb, B_DV), lambda b, h, t: (rb(b, t), h)),
            pl.BlockSpec((tb, B_DV), lambda b, h, t: (rb(b, t), OFF_GB // B_DV + h)),
            pl.BlockSpec((1, B_DV), lambda b, h, t: (0, 0)),
        ]
        args += [oprev, p, subln.reshape(1, B_DV)]
    out_specs = [pl.BlockSpec((tb, B_DV), lambda b, h, t: (rb(b, t), h))]
    out_shape = [jax.ShapeDtypeStruct((st.rows, B_V), BF16 if finalize else F32)]
    if want_state:
        out_specs.append(pl.BlockSpec((None, None, B_DK, B_DV), lambda b, h, t: (b, h, 0, 0)))
        out_shape.append(jax.ShapeDtypeStruct((batch, B_HEADS, B_DK, B_DV), F32))
    res = pl.pallas_call(
        functools.partial(_ret_kernel, bwd=bwd, tb=tb, has_init=has_init, finalize=finalize,
                          want_state=want_state),
        grid=(batch, B_HEADS, nt),
        in_specs=in_specs,
        out_specs=out_specs,
        out_shape=out_shape,
        scratch_shapes=[pltpu.VMEM((B_DK, B_DV), F32)],
        compiler_params=_cparams("parallel", "parallel", "arbitrary"),
        name="retention_bwd" if bwd else "retention_fwd",
    )(*args)
    return (res[0], res[1]) if want_state else (res[0], None)


def _retention(st_c, st_l, p_ctx, p_lat, log_f, log_b, subln):
    of_c, s_f = _retention_dir(st_c, p_ctx, log_f, False, None, None, None, True)
    ob_c, s_b = _retention_dir(st_c, p_ctx, log_b, True, None, of_c, subln, True)
    of_l, _ = _retention_dir(st_l, p_lat, log_f, False, s_f, None, None, False)
    ob_l, _ = _retention_dir(st_l, p_lat, log_b, True, s_b, of_l, subln, False)
    return ob_c, ob_l


def _dft_cos_sin(n):
    idx = np.arange(n)
    ang = 2.0 * np.pi * ((idx[:, None] * idx[None, :]) % n) / n
    return np.cos(ang), np.sin(ang)


def _fft_stage1_kernel(x_ref, m1_ref, twc_ref, tws_ref, o_ref, *, n1, n2b, cw):
    res = jnp.dot(m1_ref[...], x_ref[...], preferred_element_type=F32)
    for s in range(n2b):
        ar = res[:n1, s * cw:(s + 1) * cw]
        ai = res[n1:, s * cw:(s + 1) * cw]
        c = twc_ref[s]
        sn = tws_ref[s]
        o_ref[0, :, s * cw:(s + 1) * cw] = (ar * c + ai * sn).astype(o_ref.dtype)
        o_ref[1, :, s * cw:(s + 1) * cw] = (ai * c - ar * sn).astype(o_ref.dtype)


def _fft_final_kernel(z_ref, m2_ref, cc_ref, sc_ref, o_ref, *, k1b, n2, cw, complex_in):
    for kk in range(k1b):
        if complex_in:
            z = z_ref[:, kk].reshape(2 * n2, cw)
        else:
            z = z_ref[...]
        x = jnp.dot(m2_ref[...], z, preferred_element_type=F32)
        xr = x[:n2].astype(BF16)
        xi = x[n2:].astype(BF16)
        for g in range(C_GROUPS):
            lo = g * C_GROUP_DIM
            y = jnp.dot(xr[:, lo:lo + C_GROUP_DIM], cc_ref[...], preferred_element_type=F32)
            y += jnp.dot(xi[:, lo:lo + C_GROUP_DIM], sc_ref[...], preferred_element_type=F32)
            o_ref[:, kk * cw + lo:kk * cw + lo + C_GROUP_DIM] = y.astype(o_ref.dtype)


def _fourier(st, u):
    batch, length = st.batch, st.length
    cw = C_W
    n2 = 128 if (length % 128 == 0 and length > 256) else length
    n1 = length // n2
    cc, sc = _dft_cos_sin(C_GROUP_DIM)
    cc = jnp.asarray(cc / math.sqrt(C_GROUP_DIM), BF16)
    sc = jnp.asarray(sc / math.sqrt(C_GROUP_DIM), BF16)
    c2, s2 = _dft_cos_sin(n2)
    if n1 == 1:
        m2 = jnp.asarray(np.concatenate([c2, -s2], axis=0) / math.sqrt(n2), BF16)
        z = u.reshape(batch, n2, cw)
        z_spec = pl.BlockSpec((None, n2, cw), lambda b, i: (b, 0, 0))
        k1b = 1
    else:
        c1, s1 = _dft_cos_sin(n1)
        m1 = jnp.asarray(np.concatenate([c1, -s1], axis=0) / math.sqrt(n1), BF16)
        i2 = np.arange(n2)[:, None]
        k1 = np.arange(n1)[None, :]
        ang = 2.0 * np.pi * ((i2 * k1) % length) / length
        twc = jnp.asarray(np.cos(ang)[:, :, None], F32)
        tws = jnp.asarray(np.sin(ang)[:, :, None], F32)
        n2b = _pick(n2, (4, 2, 1))
        stage1 = pl.pallas_call(
            functools.partial(_fft_stage1_kernel, n1=n1, n2b=n2b, cw=cw),
            grid=(batch, n2 // n2b),
            in_specs=[
                pl.BlockSpec((None, n1, n2b * cw), lambda b, j: (b, 0, j)),
                pl.BlockSpec((2 * n1, n1), lambda b, j: (0, 0)),
                pl.BlockSpec((n2b, n1, 1), lambda b, j: (j, 0, 0)),
                pl.BlockSpec((n2b, n1, 1), lambda b, j: (j, 0, 0)),
            ],
            out_specs=pl.BlockSpec((None, 2, n1, n2b * cw), lambda b, j: (b, 0, 0, j)),
            out_shape=jax.ShapeDtypeStruct((batch, 2, n1, n2 * cw), BF16),
            compiler_params=_cparams("parallel", "parallel"),
            name="fft_stage1",
        )(u.reshape(batch, n1, n2 * cw), m1, twc, tws)
        m2 = jnp.asarray(np.block([[c2, s2], [-s2, c2]]) / math.sqrt(n2), BF16)
        z = stage1.reshape(batch, 2, n1, n2, cw)
        k1b = _pick(n1, (4, 2, 1))
        z_spec = pl.BlockSpec((None, 2, k1b, n2, cw), lambda b, i: (b, 0, i, 0, 0))
    out = pl.pallas_call(
        functools.partial(_fft_final_kernel, k1b=k1b, n2=n2, cw=cw, complex_in=n1 > 1),
        grid=(batch, n1 // k1b),
        in_specs=[
            z_spec,
            pl.BlockSpec(m2.shape, lambda b, i: (0, 0)),
            pl.BlockSpec(cc.shape, lambda b, i: (0, 0)),
            pl.BlockSpec(sc.shape, lambda b, i: (0, 0)),
        ],
        out_specs=pl.BlockSpec((None, n2, k1b * cw), lambda b, i: (b, 0, i)),
        out_shape=jax.ShapeDtypeStruct((batch, n2, n1 * cw), BF16),
        compiler_params=_cparams("parallel", "parallel"),
        name="fft_final",
    )(z, m2, cc, sc)
    return out.reshape(batch * length, cw)


def _convglu_kernel(g_ref, v_ref, gp_ref, gn_ref, vp_ref, vn_ref, wg_ref, wv_ref, bg_ref, bv_ref, o_ref,
                    *, tr, seq_tiles):
    i = pl.program_id(1)
    first = (i % seq_tiles) == 0
    last = (i % seq_tiles) == seq_tiles - 1
    row = lax.broadcasted_iota(jnp.int32, (tr, 1), 0)

    def conv(u_ref, p_ref, n_ref, w_ref, b_ref):
        u = u_ref[...].astype(F32)
        hb = p_ref.shape[0]
        prev_row = jnp.where(first, 0.0, p_ref[hb - 1:hb, :].astype(F32))
        next_row = jnp.where(last, 0.0, n_ref[0:1, :].astype(F32))
        up = jnp.where(row == 0, prev_row, pltpu.roll(u, 1, 0))
        dn = jnp.where(row == tr - 1, next_row, pltpu.roll(u, tr - 1, 0))
        return up * w_ref[0:1, :] + u * w_ref[1:2, :] + dn * w_ref[2:3, :] + b_ref[...]

    a = conv(g_ref, gp_ref, gn_ref, wg_ref, bg_ref)
    b = conv(v_ref, vp_ref, vn_ref, wv_ref, bv_ref)
    o_ref[...] = (a * jax.nn.sigmoid(a) * b).astype(o_ref.dtype)


def _convglu(st, u, conv_w, conv_b):
    m, f2 = u.shape
    f = f2 // 2
    tr = _pick(st.length, (512, 256, 128))
    tc = _pick(f, (1024, 512, 256, 128))
    nj = f // tc
    hb = V7X_BF16_SUBLANES
    nhb = m // hb
    rpt = tr // hb
    prev = lambda i: jnp.maximum(i * rpt - 1, 0)
    nxt = lambda i: jnp.minimum((i + 1) * rpt, nhb - 1)
    return pl.pallas_call(
        functools.partial(_convglu_kernel, tr=tr, seq_tiles=st.length // tr),
        grid=(nj, m // tr),
        in_specs=[
            pl.BlockSpec((tr, tc), lambda j, i: (i, j)),
            pl.BlockSpec((tr, tc), lambda j, i: (i, nj + j)),
            pl.BlockSpec((hb, tc), lambda j, i: (prev(i), j)),
            pl.BlockSpec((hb, tc), lambda j, i: (nxt(i), j)),
            pl.BlockSpec((hb, tc), lambda j, i: (prev(i), nj + j)),
            pl.BlockSpec((hb, tc), lambda j, i: (nxt(i), nj + j)),
            pl.BlockSpec((3, tc), lambda j, i: (0, j)),
            pl.BlockSpec((3, tc), lambda j, i: (0, nj + j)),
            pl.BlockSpec((1, tc), lambda j, i: (0, j)),
            pl.BlockSpec((1, tc), lambda j, i: (0, nj + j)),
        ],
        out_specs=pl.BlockSpec((tr, tc), lambda j, i: (i, j)),
        out_shape=jax.ShapeDtypeStruct((m, f), BF16),
        compiler_params=_cparams("parallel", "parallel"),
        name="conv_glu",
    )(u, u, u, u, u, u, conv_w, conv_w, conv_b.reshape(1, f2), conv_b.reshape(1, f2))


def _rope_tables(n_tokens):
    rows = n_tokens // GRID_W
    r, col = jnp.meshgrid(jnp.arange(rows), jnp.arange(GRID_W), indexing="ij")
    pos = jnp.stack([r.reshape(-1), col.reshape(-1)], axis=-1).astype(F32)
    n_freq = A_DIM // 4
    inv_freq = ROPE_BASE ** (-jnp.arange(n_freq, dtype=F32) / n_freq)
    ang = pos[:, :, None, None] * inv_freq
    ang = jnp.broadcast_to(ang, (n_tokens, 2, 2, n_freq)).reshape(n_tokens, A_DIM)
    sign = jnp.where((jnp.arange(A_DIM) % (A_DIM // 2)) < (A_DIM // 4), -1.0, 1.0).astype(F32)
    return jnp.cos(ang), jnp.sin(ang) * sign


def _mixer_and_ffn(st_l, st_c, cos, sin_s, x_l, x_c, wts):
    mods = wts["mods"]
    w_in = wts["w_in"].astype(BF16)
    d = x_l.shape[1]
    streams = ((st_l, x_l), (st_c, x_c))

    proj = []
    for st, x in streams:
        h = _norm_mod(st, x, wts["norm1_g"], mods, 0)
        tabs = (cos, sin_s) if not st.is_ctx else (cos[:st.length], sin_s[:st.length])
        p_mix = _proj(st, h, w_in, 0, MIX_COLS, *tabs)
        p_u = _proj(st, h, w_in, OFF_UC, C_W, *tabs)
        p_g = _proj(st, h, w_in, OFF_GATES, 3 * d, *tabs)
        proj.append((p_mix, p_u, p_g))
    (pm_l, pu_l, pg_l), (pm_c, pu_c, pg_c) = proj

    lc, ll = st_c.length, st_l.length
    oa_l = _attention(st_l, pm_l, pm_c, pm_l, wts["scal"], wts["attn_subln"], lc, ll)
    oa_c = _attention(st_c, pm_c, pm_c, None, wts["scal"], wts["attn_subln"], lc, ll)
    ob_c, ob_l = _retention(st_c, st_l, pm_c, pm_l, wts["log_f"], wts["log_b"], wts["ret_subln"])
    oc_l = _fourier(st_l, pu_l)
    oc_c = _fourier(st_c, pu_c)

    w_pa = wts["w_pa"].astype(BF16)
    w_pb = wts["w_pb"].astype(BF16)
    w_pc = wts["w_pc"].astype(BF16)
    w_out = wts["w_out"].astype(BF16)
    w_up = wts["w_up"].astype(BF16)
    w_down = wts["w_down"].astype(BF16)

    outs = []
    for st, x, oa, ob, oc, pg in ((st_l, x_l, oa_l, ob_l, oc_l, pg_l), (st_c, x_c, oa_c, ob_c, oc_c, pg_c)):
        y = _merge(st, oa, ob, oc, w_pa, w_pb, w_pc, pg)
        x1 = _res_matmul(st, y, w_out, x, mods, 2)
        h2 = _norm_mod(st, x1, wts["norm2_g"], mods, 3)
        u = _matmul(st, h2, w_up, BF16)
        act = _convglu(st, u, wts["conv_w"], wts["conv_b"])
        outs.append(_res_matmul(st, act, w_down, x1, mods, 5))
    return outs[0], outs[1]


def kernel(x, c, ctx, c_ctx, w_ada_down, w_ada_up, b_ada, norm1_g, norm2_g, w_in, lam_q1, lam_k1, lam_q2,
           lam_k2, attn_subln, ret_decay_fwd, ret_decay_bwd, ret_subln, w_pa, w_pb, w_pc, w_out, w_up,
           conv_w, conv_b, w_down, final_g):
    batch, seq, d = x.shape
    lc = ctx.shape[1]
    depth = w_in.shape[0]
    assert batch + 1 <= V7X_MOD_ROWS
    st_l = _Stream(batch, seq, False)
    st_c = _Stream(batch, lc, True)

    cond = jnp.zeros((V7X_MOD_ROWS, d), F32).at[:batch].set(c.astype(F32)).at[batch].set(c_ctx.astype(F32))
    mods = _ada_params(cond, w_ada_down, w_ada_up, b_ada)

    lam_init = jnp.asarray([0.8 - 0.6 * math.exp(-0.3 * l) for l in range(depth)], F32)
    lam = (jnp.exp(jnp.sum(lam_q1.astype(F32) * lam_k1.astype(F32), axis=-1))
           - jnp.exp(jnp.sum(lam_q2.astype(F32) * lam_k2.astype(F32), axis=-1)) + lam_init)
    scal = jnp.stack([lam, 1.0 - lam_init], axis=-1)
    log_f = jax.nn.log_sigmoid(ret_decay_fwd.astype(F32))
    log_b = jax.nn.log_sigmoid(ret_decay_bwd.astype(F32))
    cos, sin_s = _rope_tables(seq)

    xs = dict(mods=mods, scal=scal, log_f=log_f, log_b=log_b, norm1_g=norm1_g, norm2_g=norm2_g, w_in=w_in,
              attn_subln=attn_subln, ret_subln=ret_subln, w_pa=w_pa, w_pb=w_pb, w_pc=w_pc, w_out=w_out,
              w_up=w_up, conv_w=conv_w, conv_b=conv_b, w_down=w_down)

    def layer(carry, wts):
        x_l, x_c = carry
        return _mixer_and_ffn(st_l, st_c, cos, sin_s, x_l, x_c, wts), None

    (x_l, _), _ = lax.scan(layer, (x.reshape(batch * seq, d), ctx.reshape(batch * lc, d)), xs)
    return _final_norm(x_l, final_g).reshape(batch, seq, d)
```

```python
import functools
import math

import numpy as np
import jax
import jax.numpy as jnp
from jax import lax
from jax.experimental import pallas as pl
from jax.experimental.pallas import tpu as pltpu

F32 = jnp.float32
BF16 = jnp.bfloat16

GRID_W = 64
N_MOD = 6
A_HEADS = 8
A_DIM = 128
B_HEADS = 8
B_DK = 128
B_DV = 256
C_GROUPS = 4
C_GROUP_DIM = 512
ROPE_BASE = 10000.0
NORM_EPS = 1e-6
LOG2_E = math.log2(math.e)
A_QK =A_HEADS * 2 * A_DIM
A_V = A_HEADS * 2 * A_DIM
B_QK = B_HEADS * B_DK
B_V = B_HEADS * B_DV
C_W = C_GROUPS * C_GROUP_DIM
OFF_KA = 0
OFF_VA = OFF_KA + A_QK
OFF_KB = OFF_VA + A_V
OFF_VB = OFF_KB + B_QK
OFF_QA = OFF_VB + B_V
OFF_QB = OFF_QA + A_QK
OFF_GB = OFF_QB + B_QK
OFF_UC = OFF_GB + B_V
OFF_GATES = OFF_UC + C_W

V7X_LANES = 128
V7X_BF16_SUBLANES = 16
V7X_MOD_ROWS = 8
V7X_VMEM_LIMIT_BYTES = 56 * 1024 * 1024


def _cparams(*sem):
    return pltpu.CompilerParams(dimension_semantics=sem, vmem_limit_bytes=V7X_VMEM_LIMIT_BYTES)


def _pick(n, prefs):
    for p in prefs:
        if n % p == 0:
            return p
    return n


def _smem():
    return pl.BlockSpec(memory_space=pltpu.SMEM)


def _ada_kernel(cond_ref, wd_ref, wu_ref, b_ref, o_ref, t_ref):
    @pl.when((pl.program_id(1) == 0) & (pl.program_id(2) == 0))
    def _():
        cnd = cond_ref[...]
        s = cnd * jax.nn.sigmoid(cnd)
        t_ref[...] = jnp.dot(s.astype(BF16), wd_ref[...].astype(BF16), preferred_element_type=F32)

    o_ref[...] = jnp.dot(t_ref[...].astype(BF16), wu_ref[...].astype(BF16),
                         preferred_element_type=F32) + b_ref[...]


def _ada_params(cond, w_down, w_up, b):
    depth, d, r = w_down.shape
    tn = _pick(d, (2048, 1024, 512, 256, 128))
    nj = d // tn
    out = pl.pallas_call(
        _ada_kernel,
        grid=(depth, N_MOD, nj),
        in_specs=[
            pl.BlockSpec((V7X_MOD_ROWS, d), lambda l, k, j: (0, 0)),
            pl.BlockSpec((None, d, r), lambda l, k, j: (l, 0, 0)),
            pl.BlockSpec((None, r, tn), lambda l, k, j: (l, 0, k * nj + j)),
            pl.BlockSpec((None, 1, tn), lambda l, k, j: (l, 0, k * nj + j)),
        ],
        out_specs=pl.BlockSpec((None, None, V7X_MOD_ROWS, tn), lambda l, k, j: (l, k, 0, j)),
        out_shape=jax.ShapeDtypeStruct((depth, N_MOD, V7X_MOD_ROWS, d), F32),
        scratch_shapes=[pltpu.VMEM((V7X_MOD_ROWS, r), F32)],
        compiler_params=_cparams("arbitrary", "arbitrary", "arbitrary"),
        name="ada_params",
    )(cond, w_down, w_up, b.reshape(depth, 1, N_MOD * d))
    return out.reshape(depth, N_MOD, V7X_MOD_ROWS, 1, d)


class _Stream:
    def __init__(self, batch, length, is_ctx):
        self.batch = batch
        self.length = length
        self.is_ctx = is_ctx
        self.rows = batch * length

    def seg(self, tile_rows):
        if self.is_ctx:
            return lambda i: self.batch
        per = self.length // tile_rows
        return lambda i: i // per


def _norm_mod_kernel(x_ref, g_ref, sh_ref, sc_ref, o_ref):
    x = x_ref[...]
    y = x * lax.rsqrt(jnp.mean(x * x, axis=-1, keepdims=True) + NORM_EPS) * g_ref[...]
    o_ref[...] = (y * (1.0 + sc_ref[...]) + sh_ref[...]).astype(o_ref.dtype)


def _norm_kernel(x_ref, g_ref, o_ref):
    x = x_ref[...]
    y = x * lax.rsqrt(jnp.mean(x * x, axis=-1, keepdims=True) + NORM_EPS) * g_ref[...]
    o_ref[...] = y.astype(o_ref.dtype)


def _norm_mod(st, x, g, mods, k_shift):
    m, d = x.shape
    tr = _pick(st.length, (256, 128, 64, 32, 16))
    seg = st.seg(tr)
    return pl.pallas_call(
        _norm_mod_kernel,
        grid=(m // tr,),
        in_specs=[
            pl.BlockSpec((tr, d), lambda i: (i, 0)),
            pl.BlockSpec((1, d), lambda i: (0, 0)),
            pl.BlockSpec((None, None, 1, d), lambda i: (k_shift, seg(i), 0, 0)),
            pl.BlockSpec((None, None, 1, d), lambda i: (k_shift + 1, seg(i), 0, 0)),
        ],
        out_specs=pl.BlockSpec((tr, d), lambda i: (i, 0)),
        out_shape=jax.ShapeDtypeStruct((m, d), BF16),
        compiler_params=_cparams("parallel"),
        name="norm_mod",
    )(x, g.reshape(1, d), mods, mods)


def _final_norm(x, g):
    m, d = x.shape
    tr = _pick(m, (256, 128, 64, 32, 16, 8))
    return pl.pallas_call(
        _norm_kernel,
        grid=(m // tr,),
        in_specs=[pl.BlockSpec((tr, d), lambda i: (i, 0)), pl.BlockSpec((1, d), lambda i: (0, 0))],
        out_specs=pl.BlockSpec((tr, d), lambda i: (i, 0)),
        out_shape=jax.ShapeDtypeStruct((m, d), F32),
        compiler_params=_cparams("parallel"),
        name="final_norm",
    )(x, g.reshape(1, d))


def _sigmoid(x):
    return 0.5 * jnp.tanh(0.5 * x) + 0.5


def _proj_kernel(*refs, mode, scale):
    if mode == "rope":
        h_ref, w_ref, cos_ref, sin_ref, o_ref = refs
    else:
        h_ref, w_ref, o_ref = refs
    acc = jnp.dot(h_ref[...], w_ref[...], preferred_element_type=F32)
    if mode == "rope":
        cos = cos_ref[...] * scale
        sin = sin_ref[...] * scale
        lane = lax.broadcasted_iota(jnp.int32, cos.shape, 1)
        first_half = (lane % (A_DIM // 2)) < (A_DIM // 4)
        for c in range(acc.shape[1] // A_DIM):
            xc = acc[:, c * A_DIM:(c + 1) * A_DIM]
            rot = jnp.where(first_half, pltpu.roll(xc, A_DIM - A_DIM // 4, 1), pltpu.roll(xc, A_DIM // 4, 1))
            o_ref[:, c * A_DIM:(c + 1) * A_DIM] = (xc * cos + rot * sin).astype(o_ref.dtype)
    elif mode == "silu":
        o_ref[...] = (acc * _sigmoid(acc)).astype(o_ref.dtype)
    elif mode == "sigmoid":
        o_ref[...] = _sigmoid(acc).astype(o_ref.dtype)
    elif scale != 1.0:
        o_ref[...] = (acc * scale).astype(o_ref.dtype)
    else:
        o_ref[...] = acc.astype(o_ref.dtype)


def _proj_sections(d):
    return dict(
        ka=(OFF_KA, A_QK, "rope", 1.0), va=(OFF_VA, A_V, "plain", 1.0),
        kb=(OFF_KB, B_QK, "rope", B_DK ** -0.5), vb=(OFF_VB, B_V, "plain", 1.0),
        qa=(OFF_QA, A_QK, "rope", A_DIM ** -0.5 * LOG2_E), qb=(OFF_QB, B_QK, "rope", 1.0),
        gb=(OFF_GB, B_V, "silu", 1.0), uc=(OFF_UC, C_W, "plain", 1.0),
        gates=(OFF_GATES, 3 * d, "sigmoid", 1.0))


def _proj(st, h, w, section, cos, sin_signed):
    col0, ncols, mode, scale = section
    if mode == "rope" and st.is_ctx:
        mode = "plain"
    m, k = h.shape
    tm = _pick(st.length, (1024, 512, 256, 128))
    tn = _pick(ncols, (1024, 512, 256, 128))
    per = st.length // tm
    j0 = col0 // tn
    in_specs = [pl.BlockSpec((tm, k), lambda j, i: (i, 0)), pl.BlockSpec((k, tn), lambda j, i: (0, j0 + j))]
    args = [h, w]
    if mode == "rope":
        in_specs += [pl.BlockSpec((tm, A_DIM), lambda j, i: (i % per, 0))] * 2
        args += [cos, sin_signed]
    return pl.pallas_call(
        functools.partial(_proj_kernel, mode=mode, scale=scale),
        grid=(ncols // tn, m // tm),
        in_specs=in_specs,
        out_specs=pl.BlockSpec((tm, tn), lambda j, i: (i, j)),
        out_shape=jax.ShapeDtypeStruct((m, ncols), BF16),
        compiler_params=_cparams("parallel", "parallel"),
        name="in_proj_" + mode,
    )(*args)


def _mm_kernel(a_ref, w_ref, o_ref):
    o_ref[...] = jnp.dot(a_ref[...], w_ref[...], preferred_element_type=F32).astype(o_ref.dtype)


def _matmul(st, a, w, out_dtype):
    m, k = a.shape
    n = w.shape[1]
    tm = _pick(st.length, (1024, 512, 256, 128))
    tn = _pick(n, (1024, 512, 256, 128))
    return pl.pallas_call(
        _mm_kernel,
        grid=(n // tn, m // tm),
        in_specs=[pl.BlockSpec((tm, k), lambda j, i: (i, 0)), pl.BlockSpec((k, tn), lambda j, i: (0, j))],
        out_specs=pl.BlockSpec((tm, tn), lambda j, i: (i, j)),
        out_shape=jax.ShapeDtypeStruct((m, n), out_dtype),
        compiler_params=_cparams("parallel", "parallel"),
        name="ffn_up",
    )(a, w)


def _res_mm_kernel(a_ref, w_ref, x_ref, g_ref, o_ref):
    y = jnp.dot(a_ref[...], w_ref[...], preferred_element_type=F32)
    o_ref[...] = x_ref[...] + g_ref[...] * y


def _res_matmul(st, a, w, x, mods, k_gate):
    m, k = a.shape
    n = w.shape[1]
    tm = _pick(st.length, (1024, 512, 256, 128))
    tn = _pick(n, (512, 256, 128))
    seg = st.seg(tm)
    return pl.pallas_call(
        _res_mm_kernel,
        grid=(n // tn, m // tm),
        in_specs=[
            pl.BlockSpec((tm, k), lambda j, i: (i, 0)),
            pl.BlockSpec((k, tn), lambda j, i: (0, j)),
            pl.BlockSpec((tm, tn), lambda j, i: (i, j)),
            pl.BlockSpec((None, None, 1, tn), lambda j, i: (k_gate, seg(i), 0, j)),
        ],
        out_specs=pl.BlockSpec((tm, tn), lambda j, i: (i, j)),
        out_shape=jax.ShapeDtypeStruct((m, n), F32),
        compiler_params=_cparams("parallel", "parallel"),
        name="res_matmul",
    )(a, w, x, mods)


def _merge_kernel(oa_ref, ob_ref, oc_ref, wa_ref, wb_ref, wc_ref, ga_ref, gb_ref, gc_ref, o_ref):
    y = ga_ref[...].astype(F32) * jnp.dot(oa_ref[...], wa_ref[...], preferred_element_type=F32)
    y += gb_ref[...].astype(F32) * jnp.dot(ob_ref[...], wb_ref[...], preferred_element_type=F32)
    y += gc_ref[...].astype(F32) * jnp.dot(oc_ref[...], wc_ref[...], preferred_element_type=F32)
    o_ref[...] = y.astype(o_ref.dtype)


def _merge(st, oa, ob, oc, w_pa, w_pb, w_pc, gates):
    m = oa.shape[0]
    d = w_pa.shape[1]
    tm = _pick(st.length, (512, 256, 128))
    tn = _pick(d, (512, 256, 128))
    nj = d // tn
    a_spec = lambda kk: pl.BlockSpec((tm, kk), lambda j, i: (i, 0))
    w_spec = lambda kk: pl.BlockSpec((kk, tn), lambda j, i: (0, j))
    g_spec = lambda br: pl.BlockSpec((tm, tn), lambda j, i: (i, br * nj + j))
    return pl.pallas_call(
        _merge_kernel,
        grid=(nj, m // tm),
        in_specs=[a_spec(A_V), a_spec(B_V), a_spec(C_W), w_spec(A_V), w_spec(B_V), w_spec(C_W),
                  g_spec(0), g_spec(1), g_spec(2)],
        out_specs=pl.BlockSpec((tm, tn), lambda j, i: (i, j)),
        out_shape=jax.ShapeDtypeStruct((m, d), BF16),
        compiler_params=_cparams("parallel", "parallel"),
        name="merge",
    )(oa, ob, oc, w_pa, w_pb, w_pc, gates, gates, gates)


def _attn_kernel(*refs, tq, tk, n_lat):
    if n_lat:
        (scal_ref, q_ref, kc_ref, vc_ref, kl_ref, vl_ref, sub_ref, o_ref, m_ref, l_ref, acc_ref,
         s_a, s_b, s_c, p_a, p_b, p_c, al_a, al_b, al_c) = refs
    else:
        scal_ref, q_ref, kc_ref, vc_ref, sub_ref, o_ref, m_ref, l_ref, acc_ref = refs
    q = q_ref[...]
    m_ref[...] = jnp.full(m_ref.shape, -1e30, F32)
    l_ref[...] = jnp.zeros(l_ref.shape, F32)
    acc_ref[...] = jnp.zeros(acc_ref.shape, F32)
    nt_dims = (((1,), (1,)), ((), ()))

    def qk(k):
        return jnp.concatenate(
            [lax.dot_general(q[:, :A_DIM], k[:, :A_DIM], nt_dims, preferred_element_type=F32),
             lax.dot_general(q[:, A_DIM:], k[:, A_DIM:], nt_dims, preferred_element_type=F32)], axis=0)

    def softmax(s):
        m_prev = m_ref[...]
        m_next = jnp.maximum(m_prev, jnp.max(s, axis=1, keepdims=True))
        alpha = jnp.exp2(m_prev - m_next)
        tiles = [jnp.exp2(s[:, t * V7X_LANES:(t + 1) * V7X_LANES] - m_next)
                 for t in range(s.shape[1] // V7X_LANES)]
        l_ref[...] = alpha * l_ref[...] + functools.reduce(lambda a, b: a + b, tiles)
        m_ref[...] = m_next
        return jnp.concatenate(tiles, axis=1).astype(BF16), alpha

    def pv(p, alpha, v):
        acc_ref[...] = (jnp.concatenate([alpha, alpha], axis=1) * acc_ref[...]
                        + jnp.dot(p, v, preferred_element_type=F32))

    if not n_lat:
        p, alpha = softmax(qk(kc_ref[...]))
        pv(p, alpha, vc_ref[...])
    else:
        def kchunk(c):
            return kl_ref[pl.ds(pl.multiple_of(c * tk, tk), tk), :]

        def vchunk(c):
            return vl_ref[pl.ds(pl.multiple_of(c * tk, tk), tk), :]

        def step(k_next, s_next, s_cur, p_cur, al_cur, p_prev, al_prev, v_prev):
            if s_next is not None:
                s_next[...] = qk(k_next)
            p_cur[...], al_cur[...] = softmax(s_cur[...])
            if p_prev is not None:
                pv(p_prev[...], al_prev[...], v_prev)

        s_c[...] = qk(kc_ref[...])
        step(kchunk(0), s_a, s_c, p_c, al_c, None, None, None)
        step(kchunk(1), s_b, s_a, p_a, al_a, p_c, al_c, vc_ref[...])

        def body(j, carry):
            step(kchunk(2 * j + 2), s_a, s_b, p_b, al_b, p_a, al_a, vchunk(2 * j))
            step(kchunk(2 * j + 3), s_b, s_a, p_a, al_a, p_b, al_b, vchunk(2 * j + 1))
            return carry

        lax.fori_loop(0, (n_lat - 2) // 2, body, 0)
        step(None, None, s_b, p_b, al_b, p_a, al_a, vchunk(n_lat - 2))
        pv(p_b[...], al_b[...], vchunk(n_lat - 1))

    lam = scal_ref[0]
    post = scal_ref[1]
    inv_l = 1.0 / jnp.sum(l_ref[...], axis=1, keepdims=True)
    o = acc_ref[:tq] * inv_l[:tq] - lam * (acc_ref[tq:] * inv_l[tq:])
    y = o * lax.rsqrt(jnp.mean(o * o, axis=-1, keepdims=True) + NORM_EPS) * sub_ref[...] * post
    o_ref[...] = y.astype(o_ref.dtype)


def _attention(st_q, q, k_ctx, v_ctx, k_lat, v_lat, scal, subln):
    batch = st_q.batch
    hw = 2 * A_DIM
    lc = k_ctx.shape[0] // batch
    tq = _pick(st_q.length, (256, 128))
    nq = st_q.length // tq
    with_lat = k_lat is not None
    ll = k_lat.shape[0] // batch if with_lat else 0
    tk = _pick(ll, (512, 256, 128)) if with_lat else 0
    n_lat = ll // tk if with_lat else 0
    in_specs = [
        _smem(),
        pl.BlockSpec((tq, hw), lambda b, h, i: (b * nq + i, h)),
        pl.BlockSpec((lc, hw), lambda b, h, i: (b, h)),
        pl.BlockSpec((lc, hw), lambda b, h, i: (b, h)),
    ]
    args = [scal, q, k_ctx, v_ctx]
    if with_lat:
        in_specs += [pl.BlockSpec((ll, hw), lambda b, h, i: (b, h))] * 2
        args += [k_lat, v_lat]
    in_specs.append(pl.BlockSpec((1, hw), lambda b, h, i: (0, 0)))
    args.append(subln.reshape(1, hw))
    scratch = [pltpu.VMEM((2 * tq, V7X_LANES), F32), pltpu.VMEM((2 * tq, V7X_LANES), F32),
               pltpu.VMEM((2 * tq, hw), F32)]
    if with_lat:
        assert n_lat >= 2 and n_lat % 2 == 0, "latent key chunks are pipelined in pairs"
        scratch += [pltpu.VMEM((2 * tq, tk), F32)] * 2 + [pltpu.VMEM((2 * tq, lc), F32)]
        scratch += [pltpu.VMEM((2 * tq, tk), BF16)] * 2 + [pltpu.VMEM((2 * tq, lc), BF16)]
        scratch += [pltpu.VMEM((2 * tq, V7X_LANES), F32)] * 3
    return pl.pallas_call(
        functools.partial(_attn_kernel, tq=tq, tk=tk, n_lat=n_lat),
        grid=(batch, A_HEADS, nq),
        in_specs=in_specs,
        out_specs=pl.BlockSpec((tq, hw), lambda b, h, i: (b * nq + i, h)),
        out_shape=jax.ShapeDtypeStruct((st_q.rows, A_V), BF16),
        scratch_shapes=scratch,
        compiler_params=_cparams("parallel", "parallel", "parallel"),
        name="diff_attn",
    )(*args)


def _ret_kernel(*refs, bwd, tb, has_init, finalize, want_state):
    refs = list(refs)
    lg_ref, q_ref, k_ref, v_ref = refs[:4]
    pos_ = 4
    s0_ref = None
    if has_init:
        s0_ref = refs[pos_]
        pos_ += 1
    if finalize:
        oprev_ref, gate_ref, sub_ref = refs[pos_:pos_ + 3]
        pos_ += 3
    o_ref = refs[pos_]
    pos_ += 1
    sfin_ref = None
    if want_state:
        sfin_ref = refs[pos_]
        pos_ += 1
    s_ref = refs[pos_]

    h = pl.program_id(1)
    t = pl.program_id(2)
    lg = lg_ref[h]

    @pl.when(t == 0)
    def _():
        if has_init:
            s_ref[...] = s0_ref[...]
        else:
            s_ref[...] = jnp.zeros(s_ref.shape, F32)

    q = q_ref[...]
    k = k_ref[...]
    v = v_ref[...]
    ii = lax.broadcasted_iota(jnp.int32, (tb, tb), 0)
    jj = lax.broadcasted_iota(jnp.int32, (tb, tb), 1)
    rel = ((jj - ii) if bwd else (ii - jj)).astype(F32)
    decay = jnp.where(rel >= 0, jnp.exp(lg * jnp.maximum(rel, 0.0)), 0.0)
    scores = lax.dot_general(q, k, (((1,), (1,)), ((), ())), preferred_element_type=F32) * decay
    intra = jnp.dot(scores.astype(BF16), v, preferred_element_type=F32)
    pos = lax.broadcasted_iota(jnp.int32, (tb, 1), 0).astype(F32)
    q_exp = (tb - pos) if bwd else (pos + 1.0)
    k_exp = pos if bwd else (tb - 1.0 - pos)
    state = s_ref[...]
    q_dec = (q.astype(F32) * jnp.exp(lg * q_exp)).astype(BF16)
    inter = jnp.dot(q_dec, state.astype(BF16), preferred_element_type=F32)
    out = intra + inter
    k_dec = (k.astype(F32) * jnp.exp(lg * k_exp)).astype(BF16)
    kv = lax.dot_general(k_dec, v, (((0,), (0,)), ((), ())), preferred_element_type=F32)
    new_state = jnp.exp(jnp.full((1, 1), lg * tb, F32)) * state + kv
    s_ref[...] = new_state

    if finalize:
        tot = out + oprev_ref[...]
        y = tot * lax.rsqrt(jnp.mean(tot * tot, axis=-1, keepdims=True) + NORM_EPS) * sub_ref[...]
        o_ref[...] = (y * gate_ref[...].astype(F32)).astype(o_ref.dtype)
    else:
        o_ref[...] = out

    if want_state:
        @pl.when(t == pl.num_programs(2) - 1)
        def _():
            sfin_ref[...] = new_state


def _retention_dir(st, qkvg, log_g, bwd, s0, oprev, subln, want_state):
    q, k, v, gate = qkvg
    batch = st.batch
    tb = _pick(st.length, (512, 256, 128))
    nt = st.length // tb
    finalize = oprev is not None
    has_init = s0 is not None

    if bwd:
        rb = lambda b, t: b * nt + (nt - 1 - t)
    else:
        rb = lambda b, t: b * nt + t

    head_blk = lambda width: pl.BlockSpec((tb, width), lambda b, h, t: (rb(b, t), h))
    in_specs = [_smem(), head_blk(B_DK), head_blk(B_DK), head_blk(B_DV)]
    args = [log_g, q, k, v]
    if has_init:
        in_specs.append(pl.BlockSpec((None, None, B_DK, B_DV), lambda b, h, t: (b, h, 0, 0)))
        args.append(s0)
    if finalize:
        in_specs += [head_blk(B_DV), head_blk(B_DV), pl.BlockSpec((1, B_DV), lambda b, h, t: (0, 0))]
        args += [oprev, gate, subln.reshape(1, B_DV)]
    out_specs = [head_blk(B_DV)]
    out_shape = [jax.ShapeDtypeStruct((st.rows, B_V), BF16 if finalize else F32)]
    if want_state:
        out_specs.append(pl.BlockSpec((None, None, B_DK, B_DV), lambda b, h, t: (b, h, 0, 0)))
        out_shape.append(jax.ShapeDtypeStruct((batch, B_HEADS, B_DK, B_DV), F32))
    res = pl.pallas_call(
        functools.partial(_ret_kernel, bwd=bwd, tb=tb, has_init=has_init, finalize=finalize,
                          want_state=want_state),
        grid=(batch, B_HEADS, nt),
        in_specs=in_specs,
        out_specs=out_specs,
        out_shape=out_shape,
        scratch_shapes=[pltpu.VMEM((B_DK, B_DV), F32)],
        compiler_params=_cparams("parallel", "parallel", "arbitrary"),
        name="retention_bwd" if bwd else "retention_fwd",
    )(*args)
    return (res[0], res[1]) if want_state else (res[0], None)


def _retention(st_c, st_l, qkvg_c, qkvg_l, log_f, log_b, subln):
    of_c, s_f = _retention_dir(st_c, qkvg_c, log_f, False, None, None, None, True)
    ob_c, s_b = _retention_dir(st_c, qkvg_c, log_b, True, None, of_c, subln, True)
    of_l, _ = _retention_dir(st_l, qkvg_l, log_f, False, s_f, None, None, False)
    ob_l, _ = _retention_dir(st_l, qkvg_l, log_b, True, s_b, of_l, subln, False)
    return ob_c, ob_l


def _dft_cos_sin(n):
    idx = np.arange(n)
    ang = 2.0 * np.pi * ((idx[:, None] * idx[None, :]) % n) / n
    return np.cos(ang), np.sin(ang)


def _fft_stage1_kernel(x_ref, m1_ref, twc_ref, tws_ref, o_ref, *, n1, n2b, cw):
    res = jnp.dot(m1_ref[...], x_ref[...], preferred_element_type=F32)
    for s in range(n2b):
        ar = res[:n1, s * cw:(s + 1) * cw]
        ai = res[n1:, s * cw:(s + 1) * cw]
        c = twc_ref[s]
        sn = tws_ref[s]
        o_ref[0, :, s * cw:(s + 1) * cw] = (ar * c + ai * sn).astype(o_ref.dtype)
        o_ref[1, :, s * cw:(s + 1) * cw] = (ai * c - ar * sn).astype(o_ref.dtype)


def _fft_final_kernel(z_ref, m2_ref, cc_ref, sc_ref, o_ref, *, k1b, n2, cw, complex_in):
    for kk in range(k1b):
        if complex_in:
            z = z_ref[:, kk].reshape(2 * n2, cw)
        else:
            z = z_ref[...]
        x = jnp.dot(m2_ref[...], z, preferred_element_type=F32)
        xr = x[:n2].astype(BF16)
        xi = x[n2:].astype(BF16)
        for g in range(C_GROUPS):
            lo = g * C_GROUP_DIM
            y = jnp.dot(xr[:, lo:lo + C_GROUP_DIM], cc_ref[...], preferred_element_type=F32)
            y += jnp.dot(xi[:, lo:lo + C_GROUP_DIM], sc_ref[...], preferred_element_type=F32)
            o_ref[:, kk * cw + lo:kk * cw + lo + C_GROUP_DIM] = y.astype(o_ref.dtype)


def _fourier(st, u):
    batch, length = st.batch, st.length
    cw = C_W
    n2 = 128 if (length % 128 == 0 and length > 256) else length
    n1 = length // n2
    cc, sc = _dft_cos_sin(C_GROUP_DIM)
    cc = jnp.asarray(cc / math.sqrt(C_GROUP_DIM), BF16)
    sc = jnp.asarray(sc / math.sqrt(C_GROUP_DIM), BF16)
    c2, s2 = _dft_cos_sin(n2)
    if n1 == 1:
        m2 = jnp.asarray(np.concatenate([c2, -s2], axis=0) / math.sqrt(n2), BF16)
        z = u.reshape(batch, n2, cw)
        z_spec = pl.BlockSpec((None, n2, cw), lambda b, i: (b, 0, 0))
        k1b = 1
    else:
        c1, s1 = _dft_cos_sin(n1)
        m1 = jnp.asarray(np.concatenate([c1, -s1], axis=0) / math.sqrt(n1), BF16)
        i2 = np.arange(n2)[:, None]
        k1 = np.arange(n1)[None, :]
        ang = 2.0 * np.pi * ((i2 * k1) % length) / length
        twc = jnp.asarray(np.cos(ang)[:, :, None], F32)
        tws = jnp.asarray(np.sin(ang)[:, :, None], F32)
        n2b = _pick(n2, (4, 2, 1))
        stage1 = pl.pallas_call(
            functools.partial(_fft_stage1_kernel, n1=n1, n2b=n2b, cw=cw),
            grid=(batch, n2 // n2b),
            in_specs=[
                pl.BlockSpec((None, n1, n2b * cw), lambda b, j: (b, 0, j)),
                pl.BlockSpec((2 * n1, n1), lambda b, j: (0, 0)),
                pl.BlockSpec((n2b, n1, 1), lambda b, j: (j, 0, 0)),
                pl.BlockSpec((n2b, n1, 1), lambda b, j: (j, 0, 0)),
            ],
            out_specs=pl.BlockSpec((None, 2, n1, n2b * cw), lambda b, j: (b, 0, 0, j)),
            out_shape=jax.ShapeDtypeStruct((batch, 2, n1, n2 * cw), BF16),
            compiler_params=_cparams("parallel", "parallel"),
            name="fft_stage1",
        )(u.reshape(batch, n1, n2 * cw), m1, twc, tws)
        m2 = jnp.asarray(np.block([[c2, s2], [-s2, c2]]) / math.sqrt(n2), BF16)
        z = stage1.reshape(batch, 2, n1, n2, cw)
        k1b = _pick(n1, (4, 2, 1))
        z_spec = pl.BlockSpec((None, 2, k1b, n2, cw), lambda b, i: (b, 0, i, 0, 0))
    out = pl.pallas_call(
        functools.partial(_fft_final_kernel, k1b=k1b, n2=n2, cw=cw, complex_in=n1 > 1),
        grid=(batch, n1 // k1b),
        in_specs=[
            z_spec,
            pl.BlockSpec(m2.shape, lambda b, i: (0, 0)),
            pl.BlockSpec(cc.shape, lambda b, i: (0, 0)),
            pl.BlockSpec(sc.shape, lambda b, i: (0, 0)),
        ],
        out_specs=pl.BlockSpec((None, n2, k1b * cw), lambda b, i: (b, 0, i)),
        out_shape=jax.ShapeDtypeStruct((batch, n2, n1 * cw), BF16),
        compiler_params=_cparams("parallel", "parallel"),
        name="fft_final",
    )(z, m2, cc, sc)
    return out.reshape(batch * length, cw)


def _convglu_kernel(g_ref, v_ref, gp_ref, gn_ref, vp_ref, vn_ref, wg_ref, wv_ref, bg_ref, bv_ref, o_ref,
                    *, tr, seq_tiles):
    i = pl.program_id(1)
    first = (i % seq_tiles) == 0
    last = (i % seq_tiles) == seq_tiles - 1
    row = lax.broadcasted_iota(jnp.int32, (tr, 1), 0)

    def conv(u_ref, p_ref, n_ref, w_ref, b_ref):
        u = u_ref[...].astype(F32)
        hb = p_ref.shape[0]
        prev_row = jnp.where(first, 0.0, p_ref[hb - 1:hb, :].astype(F32))
        next_row = jnp.where(last, 0.0, n_ref[0:1, :].astype(F32))
        up = jnp.where(row == 0, prev_row, pltpu.roll(u, 1, 0))
        dn = jnp.where(row == tr - 1, next_row, pltpu.roll(u, tr - 1, 0))
        return up * w_ref[0:1, :] + u * w_ref[1:2, :] + dn * w_ref[2:3, :] + b_ref[...]

    a = conv(g_ref, gp_ref, gn_ref, wg_ref, bg_ref)
    b = conv(v_ref, vp_ref, vn_ref, wv_ref, bv_ref)
    o_ref[...] = (a * _sigmoid(a) * b).astype(o_ref.dtype)


def _convglu(st, u, conv_w, conv_b):
    m, f2 = u.shape
    f = f2 // 2
    tr = _pick(st.length, (512, 256, 128))
    tc = _pick(f, (1024, 512, 256, 128))
    nj = f // tc
    hb = V7X_BF16_SUBLANES
    nhb = m // hb
    rpt = tr // hb
    prev = lambda i: jnp.maximum(i * rpt - 1, 0)
    nxt = lambda i: jnp.minimum((i + 1) * rpt, nhb - 1)
    return pl.pallas_call(
        functools.partial(_convglu_kernel, tr=tr, seq_tiles=st.length // tr),
        grid=(nj, m // tr),
        in_specs=[
            pl.BlockSpec((tr, tc), lambda j, i: (i, j)),
            pl.BlockSpec((tr, tc), lambda j, i: (i, nj + j)),
            pl.BlockSpec((hb, tc), lambda j, i: (prev(i), j)),
            pl.BlockSpec((hb, tc), lambda j, i: (nxt(i), j)),
            pl.BlockSpec((hb, tc), lambda j, i: (prev(i), nj + j)),
            pl.BlockSpec((hb, tc), lambda j, i: (nxt(i), nj + j)),
            pl.BlockSpec((3, tc), lambda j, i: (0, j)),
            pl.BlockSpec((3, tc), lambda j, i: (0, nj + j)),
            pl.BlockSpec((1, tc), lambda j, i: (0, j)),
            pl.BlockSpec((1, tc), lambda j, i: (0, nj + j)),
        ],
        out_specs=pl.BlockSpec((tr, tc), lambda j, i: (i, j)),
        out_shape=jax.ShapeDtypeStruct((m, f), BF16),
        compiler_params=_cparams("parallel", "parallel"),
        name="conv_glu",
    )(u, u, u, u, u, u, conv_w, conv_w, conv_b.reshape(1, f2), conv_b.reshape(1, f2))


def _rope_tables(n_tokens):
    rows = n_tokens // GRID_W
    r, col = jnp.meshgrid(jnp.arange(rows), jnp.arange(GRID_W), indexing="ij")
    pos = jnp.stack([r.reshape(-1), col.reshape(-1)], axis=-1).astype(F32)
    n_freq = A_DIM // 4
    inv_freq = ROPE_BASE ** (-jnp.arange(n_freq, dtype=F32) / n_freq)
    ang = pos[:, :, None, None] * inv_freq
    ang = jnp.broadcast_to(ang, (n_tokens, 2, 2, n_freq)).reshape(n_tokens, A_DIM)
    sign = jnp.where((jnp.arange(A_DIM) % (A_DIM // 2)) < (A_DIM // 4), -1.0, 1.0).astype(F32)
    return jnp.cos(ang), jnp.sin(ang) * sign


def _mixer_and_ffn(st_l, st_c, cos, sin_s, x_l, x_c, wts):
    mods = wts["mods"]
    w_in = wts["w_in"].astype(BF16)
    d = x_l.shape[1]
    streams = ((st_l, x_l), (st_c, x_c))

    sections = _proj_sections(d)
    proj = []
    for st, x in streams:
        h = _norm_mod(st, x, wts["norm1_g"], mods, 0)
        proj.append({name: _proj(st, h, w_in, sec, cos, sin_s) for name, sec in sections.items()})
    pl_, pc_ = proj

    oa_l = _attention(st_l, pl_["qa"], pc_["ka"], pc_["va"], pl_["ka"], pl_["va"], wts["scal"], wts["attn_subln"])
    oa_c = _attention(st_c, pc_["qa"], pc_["ka"], pc_["va"], None, None, wts["scal"], wts["attn_subln"])
    ob_c, ob_l = _retention(st_c, st_l, tuple(pc_[n] for n in ("qb", "kb", "vb", "gb")),
                            tuple(pl_[n] for n in ("qb", "kb", "vb", "gb")),
                            wts["log_f"], wts["log_b"], wts["ret_subln"])
    oc_l = _fourier(st_l, pl_["uc"])
    oc_c = _fourier(st_c, pc_["uc"])
    pg_l, pg_c = pl_["gates"], pc_["gates"]

    w_pa = wts["w_pa"].astype(BF16)
    w_pb = wts["w_pb"].astype(BF16)
    w_pc = wts["w_pc"].astype(BF16)
    w_out = wts["w_out"].astype(BF16)
    w_up = wts["w_up"].astype(BF16)
    w_down = wts["w_down"].astype(BF16)

    outs = []
    for st, x, oa, ob, oc, pg in ((st_l, x_l, oa_l, ob_l, oc_l, pg_l), (st_c, x_c, oa_c, ob_c, oc_c, pg_c)):
        y = _merge(st, oa, ob, oc, w_pa, w_pb, w_pc, pg)
        x1 = _res_matmul(st, y, w_out, x, mods, 2)
        h2 = _norm_mod(st, x1, wts["norm2_g"], mods, 3)
        u = _matmul(st, h2, w_up, BF16)
        act = _convglu(st, u, wts["conv_w"], wts["conv_b"])
        outs.append(_res_matmul(st, act, w_down, x1, mods, 5))
    return outs[0], outs[1]


def kernel(x, c, ctx, c_ctx, w_ada_down, w_ada_up, b_ada, norm1_g, norm2_g, w_in, lam_q1, lam_k1, lam_q2,
           lam_k2, attn_subln, ret_decay_fwd, ret_decay_bwd, ret_subln, w_pa, w_pb, w_pc, w_out, w_up,
           conv_w, conv_b, w_down, final_g):
    batch, seq, d = x.shape
    lc = ctx.shape[1]
    depth = w_in.shape[0]
    assert batch + 1 <= V7X_MOD_ROWS
    st_l = _Stream(batch, seq, False)
    st_c = _Stream(batch, lc, True)

    cond = jnp.zeros((V7X_MOD_ROWS, d), F32).at[:batch].set(c.astype(F32)).at[batch].set(c_ctx.astype(F32))
    mods = _ada_params(cond, w_ada_down, w_ada_up, b_ada)

    lam_init = jnp.asarray([0.8 - 0.6 * math.exp(-0.3 * l) for l in range(depth)], F32)
    lam = (jnp.exp(jnp.sum(lam_q1.astype(F32) * lam_k1.astype(F32), axis=-1))
           - jnp.exp(jnp.sum(lam_q2.astype(F32) * lam_k2.astype(F32), axis=-1)) + lam_init)
    scal = jnp.stack([lam, 1.0 - lam_init], axis=-1)
    log_f = jax.nn.log_sigmoid(ret_decay_fwd.astype(F32))
    log_b = jax.nn.log_sigmoid(ret_decay_bwd.astype(F32))
    cos, sin_s = _rope_tables(seq)

    xs = dict(mods=mods, scal=scal, log_f=log_f, log_b=log_b, norm1_g=norm1_g, norm2_g=norm2_g, w_in=w_in,
              attn_subln=attn_subln, ret_subln=ret_subln, w_pa=w_pa, w_pb=w_pb, w_pc=w_pc, w_out=w_out,
              w_up=w_up, conv_w=conv_w, conv_b=conv_b, w_down=w_down)

    def layer(carry, wts):
        x_l, x_c = carry
        return _mixer_and_ffn(st_l, st_c, cos, sin_s, x_l, x_c, wts), None

    (x_l, _), _ = lax.scan(layer, (x.reshape(batch * seq, d), ctx.reshape(batch * lc, d)), xs)
    return _final_norm(x_l, final_g).reshape(batch, seq, d)
```

```python
import functools
import math

import numpy as np
import jax
import jax.numpy as jnp
from jax import lax
from jax.experimental import pallas as pl
from jax.experimental.pallas import tpu as pltpu

F32 = jnp.float32
BF16 = jnp.bfloat16

GRID_W = 64
N_MOD = 6
A_HEADS = 8
A_DIM = 128
B_HEADS = 8
B_DK = 128
B_DV = 256
C_GROUPS = 4
C_GROUP_DIM = 512
ROPE_BASE = 10000.0
NORM_EPS = 1e-6
LOG2_E = math.log2(math.e)
A_QK =A_HEADS * 2 * A_DIM
A_V = A_HEADS * 2 * A_DIM
B_QK = B_HEADS * B_DK
B_V = B_HEADS * B_DV
C_W = C_GROUPS * C_GROUP_DIM
OFF_KA = 0
OFF_VA = OFF_KA + A_QK
OFF_KB = OFF_VA + A_V
OFF_VB = OFF_KB + B_QK
OFF_QA = OFF_VB + B_V
OFF_QB = OFF_QA + A_QK
OFF_GB = OFF_QB + B_QK
OFF_UC = OFF_GB + B_V
OFF_GATES = OFF_UC + C_W

V7X_LANES = 128
V7X_BF16_SUBLANES = 16
V7X_MOD_ROWS = 8
V7X_VMEM_LIMIT_BYTES = 56 * 1024 * 1024


def _cparams(*sem):
    return pltpu.CompilerParams(dimension_semantics=sem, vmem_limit_bytes=V7X_VMEM_LIMIT_BYTES)


def _pick(n, prefs):
    for p in prefs:
        if n % p == 0:
            return p
    return n


def _smem():
    return pl.BlockSpec(memory_space=pltpu.SMEM)


def _ada_kernel(cond_ref, wd_ref, wu_ref, b_ref, o_ref, t_ref):
    @pl.when((pl.program_id(1) == 0) & (pl.program_id(2) == 0))
    def _():
        cnd = cond_ref[...]
        s = cnd * jax.nn.sigmoid(cnd)
        t_ref[...] = jnp.dot(s.astype(BF16), wd_ref[...].astype(BF16), preferred_element_type=F32)

    o_ref[...] = jnp.dot(t_ref[...].astype(BF16), wu_ref[...].astype(BF16),
                         preferred_element_type=F32) + b_ref[...]


def _ada_params(cond, w_down, w_up, b):
    depth, d, r = w_down.shape
    tn = _pick(d, (2048, 1024, 512, 256, 128))
    nj = d // tn
    out = pl.pallas_call(
        _ada_kernel,
        grid=(depth, N_MOD, nj),
        in_specs=[
            pl.BlockSpec((V7X_MOD_ROWS, d), lambda l, k, j: (0, 0)),
            pl.BlockSpec((None, d, r), lambda l, k, j: (l, 0, 0)),
            pl.BlockSpec((None, r, tn), lambda l, k, j: (l, 0, k * nj + j)),
            pl.BlockSpec((None, 1, tn), lambda l, k, j: (l, 0, k * nj + j)),
        ],
        out_specs=pl.BlockSpec((None, None, V7X_MOD_ROWS, tn), lambda l, k, j: (l, k, 0, j)),
        out_shape=jax.ShapeDtypeStruct((depth, N_MOD, V7X_MOD_ROWS, d), F32),
        scratch_shapes=[pltpu.VMEM((V7X_MOD_ROWS, r), F32)],
        compiler_params=_cparams("arbitrary", "arbitrary", "arbitrary"),
        name="ada_params",
    )(cond, w_down, w_up, b.reshape(depth, 1, N_MOD * d))
    return out.reshape(depth, N_MOD, V7X_MOD_ROWS, 1, d)


class _Stream:
    def __init__(self, batch, length, is_ctx):
        self.batch = batch
        self.length = length
        self.is_ctx = is_ctx
        self.rows = batch * length

    def seg(self, tile_rows):
        if self.is_ctx:
            return lambda i: self.batch
        per = self.length // tile_rows
        return lambda i: i // per


def _norm_mod_kernel(x_ref, g_ref, sh_ref, sc_ref, o_ref):
    x = x_ref[...]
    y = x * lax.rsqrt(jnp.mean(x * x, axis=-1, keepdims=True) + NORM_EPS) * g_ref[...]
    o_ref[...] = (y * (1.0 + sc_ref[...]) + sh_ref[...]).astype(o_ref.dtype)


def _norm_kernel(x_ref, g_ref, o_ref):
    x = x_ref[...]
    y = x * lax.rsqrt(jnp.mean(x * x, axis=-1, keepdims=True) + NORM_EPS) * g_ref[...]
    o_ref[...] = y.astype(o_ref.dtype)


def _norm_mod(st, x, g, mods, k_shift):
    m, d = x.shape
    tr = _pick(st.length, (256, 128, 64, 32, 16))
    seg = st.seg(tr)
    return pl.pallas_call(
        _norm_mod_kernel,
        grid=(m // tr,),
        in_specs=[
            pl.BlockSpec((tr, d), lambda i: (i, 0)),
            pl.BlockSpec((1, d), lambda i: (0, 0)),
            pl.BlockSpec((None, None, 1, d), lambda i: (k_shift, seg(i), 0, 0)),
            pl.BlockSpec((None, None, 1, d), lambda i: (k_shift + 1, seg(i), 0, 0)),
        ],
        out_specs=pl.BlockSpec((tr, d), lambda i: (i, 0)),
        out_shape=jax.ShapeDtypeStruct((m, d), BF16),
        compiler_params=_cparams("parallel"),
        name="norm_mod",
    )(x, g.reshape(1, d), mods, mods)


def _final_norm(x, g):
    m, d = x.shape
    tr = _pick(m, (256, 128, 64, 32, 16, 8))
    return pl.pallas_call(
        _norm_kernel,
        grid=(m // tr,),
        in_specs=[pl.BlockSpec((tr, d), lambda i: (i, 0)), pl.BlockSpec((1, d), lambda i: (0, 0))],
        out_specs=pl.BlockSpec((tr, d), lambda i: (i, 0)),
        out_shape=jax.ShapeDtypeStruct((m, d), F32),
        compiler_params=_cparams("parallel"),
        name="final_norm",
    )(x, g.reshape(1, d))


def _sigmoid(x):
    return 0.5 * jnp.tanh(0.5 * x) + 0.5


def _proj_kernel(*refs, mode, scale):
    if mode == "rope":
        h_ref, w_ref, cos_ref, sin_ref, o_ref = refs
    else:
        h_ref, w_ref, o_ref = refs
    acc = jnp.dot(h_ref[...], w_ref[...], preferred_element_type=F32)
    if mode == "rope":
        cos = cos_ref[...] * scale
        sin = sin_ref[...] * scale
        lane = lax.broadcasted_iota(jnp.int32, cos.shape, 1)
        first_half = (lane % (A_DIM // 2)) < (A_DIM // 4)
        for c in range(acc.shape[1] // A_DIM):
            xc = acc[:, c * A_DIM:(c + 1) * A_DIM]
            rot = jnp.where(first_half, pltpu.roll(xc, A_DIM - A_DIM // 4, 1), pltpu.roll(xc, A_DIM // 4, 1))
            o_ref[:, c * A_DIM:(c + 1) * A_DIM] = (xc * cos + rot * sin).astype(o_ref.dtype)
    elif mode == "silu":
        o_ref[...] = (acc * _sigmoid(acc)).astype(o_ref.dtype)
    elif mode == "sigmoid":
        o_ref[...] = _sigmoid(acc).astype(o_ref.dtype)
    elif scale != 1.0:
        o_ref[...] = (acc * scale).astype(o_ref.dtype)
    else:
        o_ref[...] = acc.astype(o_ref.dtype)


def _proj_sections(d):
    return dict(
        ka=(OFF_KA, A_QK, "rope", 1.0), va=(OFF_VA, A_V, "plain", 1.0),
        kb=(OFF_KB, B_QK, "rope", B_DK ** -0.5), vb=(OFF_VB, B_V, "plain", 1.0),
        qa=(OFF_QA, A_QK, "rope", A_DIM ** -0.5 * LOG2_E), qb=(OFF_QB, B_QK, "rope", 1.0),
        gb=(OFF_GB, B_V, "silu", 1.0), uc=(OFF_UC, C_W, "plain", 1.0),
        gates=(OFF_GATES, 3 * d, "sigmoid", 1.0))


def _proj(st, h, w, section, cos, sin_signed):
    col0, ncols, mode, scale = section
    if mode == "rope" and st.is_ctx:
        mode = "plain"
    m, k = h.shape
    tm = _pick(st.length, (1024, 512, 256, 128))
    tn = _pick(ncols, (1024, 512, 256, 128))
    per = st.length // tm
    j0 = col0 // tn
    in_specs = [pl.BlockSpec((tm, k), lambda j, i: (i, 0)), pl.BlockSpec((k, tn), lambda j, i: (0, j0 + j))]
    args = [h, w]
    if mode == "rope":
        in_specs += [pl.BlockSpec((tm, A_DIM), lambda j, i: (i % per, 0))] * 2
        args += [cos, sin_signed]
    return pl.pallas_call(
        functools.partial(_proj_kernel, mode=mode, scale=scale),
        grid=(ncols // tn, m // tm),
        in_specs=in_specs,
        out_specs=pl.BlockSpec((tm, tn), lambda j, i: (i, j)),
        out_shape=jax.ShapeDtypeStruct((m, ncols), BF16),
        compiler_params=_cparams("parallel", "parallel"),
        name="in_proj_" + mode,
    )(*args)


def _res_mm_kernel(a_ref, w_ref, x_ref, g_ref, o_ref):
    y = jnp.dot(a_ref[...], w_ref[...], preferred_element_type=F32)
    o_ref[...] = x_ref[...] + g_ref[...] * y


def _res_matmul(st, a, w, x, mods, k_gate):
    m, k = a.shape
    n = w.shape[1]
    tm = _pick(st.length, (1024, 512, 256, 128))
    tn = _pick(n, (512, 256, 128))
    seg = st.seg(tm)
    return pl.pallas_call(
        _res_mm_kernel,
        grid=(n // tn, m // tm),
        in_specs=[
            pl.BlockSpec((tm, k), lambda j, i: (i, 0)),
            pl.BlockSpec((k, tn), lambda j, i: (0, j)),
            pl.BlockSpec((tm, tn), lambda j, i: (i, j)),
            pl.BlockSpec((None, None, 1, tn), lambda j, i: (k_gate, seg(i), 0, j)),
        ],
        out_specs=pl.BlockSpec((tm, tn), lambda j, i: (i, j)),
        out_shape=jax.ShapeDtypeStruct((m, n), F32),
        compiler_params=_cparams("parallel", "parallel"),
        name="res_matmul",
    )(a, w, x, mods)


def _merge_kernel(oa_ref, ob_ref, oc_ref, wa_ref, wb_ref, wc_ref, ga_ref, gb_ref, gc_ref, o_ref):
    y = ga_ref[...].astype(F32) * jnp.dot(oa_ref[...], wa_ref[...], preferred_element_type=F32)
    y += gb_ref[...].astype(F32) * jnp.dot(ob_ref[...], wb_ref[...], preferred_element_type=F32)
    y += gc_ref[...].astype(F32) * jnp.dot(oc_ref[...], wc_ref[...], preferred_element_type=F32)
    o_ref[...] = y.astype(o_ref.dtype)


def _merge(st, oa, ob, oc, w_pa, w_pb, w_pc, gates):
    m = oa.shape[0]
    d = w_pa.shape[1]
    tm = _pick(st.length, (512, 256, 128))
    tn = _pick(d, (512, 256, 128))
    nj = d // tn
    a_spec = lambda kk: pl.BlockSpec((tm, kk), lambda j, i: (i, 0))
    w_spec = lambda kk: pl.BlockSpec((kk, tn), lambda j, i: (0, j))
    g_spec = lambda br: pl.BlockSpec((tm, tn), lambda j, i: (i, br * nj + j))
    return pl.pallas_call(
        _merge_kernel,
        grid=(nj, m // tm),
        in_specs=[a_spec(A_V), a_spec(B_V), a_spec(C_W), w_spec(A_V), w_spec(B_V), w_spec(C_W),
                  g_spec(0), g_spec(1), g_spec(2)],
        out_specs=pl.BlockSpec((tm, tn), lambda j, i: (i, j)),
        out_shape=jax.ShapeDtypeStruct((m, d), BF16),
        compiler_params=_cparams("parallel", "parallel"),
        name="merge",
    )(oa, ob, oc, w_pa, w_pb, w_pc, gates, gates, gates)


def _attn_kernel(*refs, tq, tk, n_lat):
    if n_lat:
        (scal_ref, q_ref, kc_ref, vc_ref, kl_ref, vl_ref, sub_ref, o_ref, m_ref, l_ref, acc_ref,
         s_a, s_b, s_c, p_a, p_b, p_c, al_a, al_b, al_c) = refs
    else:
        scal_ref, q_ref, kc_ref, vc_ref, sub_ref, o_ref, m_ref, l_ref, acc_ref = refs
    q = q_ref[...]
    m_ref[...] = jnp.full(m_ref.shape, -1e30, F32)
    l_ref[...] = jnp.zeros(l_ref.shape, F32)
    acc_ref[...] = jnp.zeros(acc_ref.shape, F32)
    nt_dims = (((1,), (1,)), ((), ()))

    def qk(k):
        return jnp.concatenate(
            [lax.dot_general(q[:, :A_DIM], k[:, :A_DIM], nt_dims, preferred_element_type=F32),
             lax.dot_general(q[:, A_DIM:], k[:, A_DIM:], nt_dims, preferred_element_type=F32)], axis=0)

    def softmax(s):
        m_prev = m_ref[...]
        m_next = jnp.maximum(m_prev, jnp.max(s, axis=1, keepdims=True))
        alpha = jnp.exp2(m_prev - m_next)
        tiles = [jnp.exp2(s[:, t * V7X_LANES:(t + 1) * V7X_LANES] - m_next)
                 for t in range(s.shape[1] // V7X_LANES)]
        l_ref[...] = alpha * l_ref[...] + functools.reduce(lambda a, b: a + b, tiles)
        m_ref[...] = m_next
        return jnp.concatenate(tiles, axis=1).astype(BF16), alpha

    def pv(p, alpha, v):
        acc_ref[...] = (jnp.concatenate([alpha, alpha], axis=1) * acc_ref[...]
                        + jnp.dot(p, v, preferred_element_type=F32))

    if not n_lat:
        p, alpha = softmax(qk(kc_ref[...]))
        pv(p, alpha, vc_ref[...])
    else:
        def kchunk(c):
            return kl_ref[pl.ds(pl.multiple_of(c * tk, tk), tk), :]

        def vchunk(c):
            return vl_ref[pl.ds(pl.multiple_of(c * tk, tk), tk), :]

        def step(k_next, s_next, s_cur, p_cur, al_cur, p_prev, al_prev, v_prev):
            if s_next is not None:
                s_next[...] = qk(k_next)
            p_cur[...], al_cur[...] = softmax(s_cur[...])
            if p_prev is not None:
                pv(p_prev[...], al_prev[...], v_prev)

        s_c[...] = qk(kc_ref[...])
        step(kchunk(0), s_a, s_c, p_c, al_c, None, None, None)
        step(kchunk(1), s_b, s_a, p_a, al_a, p_c, al_c, vc_ref[...])

        def body(j, carry):
            step(kchunk(2 * j + 2), s_a, s_b, p_b, al_b, p_a, al_a, vchunk(2 * j))
            step(kchunk(2 * j + 3), s_b, s_a, p_a, al_a, p_b, al_b, vchunk(2 * j + 1))
            return carry

        lax.fori_loop(0, (n_lat - 2) // 2, body, 0, unroll=True)
        step(None, None, s_b, p_b, al_b, p_a, al_a, vchunk(n_lat - 2))
        pv(p_b[...], al_b[...], vchunk(n_lat - 1))

    lam = scal_ref[0]
    post = scal_ref[1]
    inv_l = 1.0 / jnp.sum(l_ref[...], axis=1, keepdims=True)
    o = acc_ref[:tq] * inv_l[:tq] - lam * (acc_ref[tq:] * inv_l[tq:])
    y = o * lax.rsqrt(jnp.mean(o * o, axis=-1, keepdims=True) + NORM_EPS) * sub_ref[...] * post
    o_ref[...] = y.astype(o_ref.dtype)


def _attention(st_q, q, k_ctx, v_ctx, k_lat, v_lat, scal, subln):
    batch = st_q.batch
    hw = 2 * A_DIM
    lc = k_ctx.shape[0] // batch
    tq = _pick(st_q.length, (256, 128))
    nq = st_q.length // tq
    with_lat = k_lat is not None
    ll = k_lat.shape[0] // batch if with_lat else 0
    tk = _pick(ll, (512, 256, 128)) if with_lat else 0
    n_lat = ll // tk if with_lat else 0
    in_specs = [
        _smem(),
        pl.BlockSpec((tq, hw), lambda b, h, i: (b * nq + i, h)),
        pl.BlockSpec((lc, hw), lambda b, h, i: (b, h)),
        pl.BlockSpec((lc, hw), lambda b, h, i: (b, h)),
    ]
    args = [scal, q, k_ctx, v_ctx]
    if with_lat:
        in_specs += [pl.BlockSpec((ll, hw), lambda b, h, i: (b, h))] * 2
        args += [k_lat, v_lat]
    in_specs.append(pl.BlockSpec((1, hw), lambda b, h, i: (0, 0)))
    args.append(subln.reshape(1, hw))
    scratch = [pltpu.VMEM((2 * tq, V7X_LANES), F32), pltpu.VMEM((2 * tq, V7X_LANES), F32),
               pltpu.VMEM((2 * tq, hw), F32)]
    if with_lat:
        assert n_lat >= 2 and n_lat % 2 == 0, "latent key chunks are pipelined in pairs"
        scratch += [pltpu.VMEM((2 * tq, tk), F32)] * 2 + [pltpu.VMEM((2 * tq, lc), F32)]
        scratch += [pltpu.VMEM((2 * tq, tk), BF16)] * 2 + [pltpu.VMEM((2 * tq, lc), BF16)]
        scratch += [pltpu.VMEM((2 * tq, V7X_LANES), F32)] * 3
    return pl.pallas_call(
        functools.partial(_attn_kernel, tq=tq, tk=tk, n_lat=n_lat),
        grid=(batch, A_HEADS, nq),
        in_specs=in_specs,
        out_specs=pl.BlockSpec((tq, hw), lambda b, h, i: (b * nq + i, h)),
        out_shape=jax.ShapeDtypeStruct((st_q.rows, A_V), BF16),
        scratch_shapes=scratch,
        compiler_params=_cparams("parallel", "parallel", "parallel"),
        name="diff_attn",
    )(*args)


def _ret_kernel(*refs, bwd, tb, has_init, finalize, want_state):
    refs = list(refs)
    lg_ref, q_ref, k_ref, v_ref = refs[:4]
    pos_ = 4
    s0_ref = None
    if has_init:
        s0_ref = refs[pos_]
        pos_ += 1
    if finalize:
        oprev_ref, gate_ref, sub_ref = refs[pos_:pos_ + 3]
        pos_ += 3
    o_ref = refs[pos_]
    pos_ += 1
    sfin_ref = None
    if want_state:
        sfin_ref = refs[pos_]
        pos_ += 1
    s_ref, d_ref = refs[pos_:pos_ + 2]

    t = pl.program_id(1)

    @pl.when(t == 0)
    def _():
        if has_init:
            s_ref[...] = s0_ref[...]
        else:
            s_ref[...] = jnp.zeros(s_ref.shape, F32)
        ii = lax.broadcasted_iota(jnp.int32, (tb, tb), 0)
        jj = lax.broadcasted_iota(jnp.int32, (tb, tb), 1)
        rel = ((jj - ii) if bwd else (ii - jj)).astype(F32)
        for h in range(B_HEADS):
            d_ref[h] = jnp.where(rel >= 0, jnp.exp(lg_ref[h] * jnp.maximum(rel, 0.0)), 0.0)

    pos = lax.broadcasted_iota(jnp.int32, (tb, 1), 0).astype(F32)
    q_exp = (tb - pos) if bwd else (pos + 1.0)
    k_exp = pos if bwd else (tb - 1.0 - pos)
    for h in range(B_HEADS):
        lg = lg_ref[h]
        q = q_ref[:, h * B_DK:(h + 1) * B_DK]
        k = k_ref[:, h * B_DK:(h + 1) * B_DK]
        v = v_ref[:, h * B_DV:(h + 1) * B_DV]
        scores = lax.dot_general(q, k, (((1,), (1,)), ((), ())), preferred_element_type=F32) * d_ref[h]
        intra = jnp.dot(scores.astype(BF16), v, preferred_element_type=F32)
        state = s_ref[h]
        q_dec = (q.astype(F32) * jnp.exp(lg * q_exp)).astype(BF16)
        out = intra + jnp.dot(q_dec, state.astype(BF16), preferred_element_type=F32)
        k_dec = (k.astype(F32) * jnp.exp(lg * k_exp)).astype(BF16)
        kv = lax.dot_general(k_dec, v, (((0,), (0,)), ((), ())), preferred_element_type=F32)
        s_ref[h] = jnp.exp(jnp.full((1, 1), lg * tb, F32)) * state + kv
        if finalize:
            tot = out + oprev_ref[:, h * B_DV:(h + 1) * B_DV]
            y = tot * lax.rsqrt(jnp.mean(tot * tot, axis=-1, keepdims=True) + NORM_EPS) * sub_ref[...]
            o_ref[:, h * B_DV:(h + 1) * B_DV] = (
                y * gate_ref[:, h * B_DV:(h + 1) * B_DV].astype(F32)).astype(o_ref.dtype)
        else:
            o_ref[:, h * B_DV:(h + 1) * B_DV] = out

    if want_state:
        @pl.when(t == pl.num_programs(1) - 1)
        def _():
            sfin_ref[...] = s_ref[...]


def _retention_dir(st, qkvg, log_g, bwd, s0, oprev, subln, want_state):
    q, k, v, gate = qkvg
    batch = st.batch
    tb = _pick(st.length, (512, 256, 128))
    nt = st.length // tb
    finalize = oprev is not None
    has_init = s0 is not None

    if bwd:
        rb = lambda b, t: b * nt + (nt - 1 - t)
    else:
        rb = lambda b, t: b * nt + t

    row_blk = lambda width: pl.BlockSpec((tb, width), lambda b, t: (rb(b, t), 0))
    state_blk = pl.BlockSpec((None, B_HEADS, B_DK, B_DV), lambda b, t: (b, 0, 0, 0))
    in_specs = [_smem(), row_blk(B_QK), row_blk(B_QK), row_blk(B_V)]
    args = [log_g, q, k, v]
    if has_init:
        in_specs.append(state_blk)
        args.append(s0)
    if finalize:
        in_specs += [row_blk(B_V), row_blk(B_V), pl.BlockSpec((1, B_DV), lambda b, t: (0, 0))]
        args += [oprev, gate, subln.reshape(1, B_DV)]
    out_specs = [row_blk(B_V)]
    out_shape = [jax.ShapeDtypeStruct((st.rows, B_V), BF16 if finalize else F32)]
    if want_state:
        out_specs.append(state_blk)
        out_shape.append(jax.ShapeDtypeStruct((batch, B_HEADS, B_DK, B_DV), F32))
    res = pl.pallas_call(
        functools.partial(_ret_kernel, bwd=bwd, tb=tb, has_init=has_init, finalize=finalize,
                          want_state=want_state),
        grid=(batch, nt),
        in_specs=in_specs,
        out_specs=out_specs,
        out_shape=out_shape,
        scratch_shapes=[pltpu.VMEM((B_HEADS, B_DK, B_DV), F32), pltpu.VMEM((B_HEADS, tb, tb), F32)],
        compiler_params=_cparams("parallel", "arbitrary"),
        name="retention_bwd" if bwd else "retention_fwd",
    )(*args)
    return (res[0], res[1]) if want_state else (res[0], None)


def _retention(st_c, st_l, qkvg_c, qkvg_l, log_f, log_b, subln):
    of_c, s_f = _retention_dir(st_c, qkvg_c, log_f, False, None, None, None, True)
    ob_c, s_b = _retention_dir(st_c, qkvg_c, log_b, True, None, of_c, subln, True)
    of_l, _ = _retention_dir(st_l, qkvg_l, log_f, False, s_f, None, None, False)
    ob_l, _ = _retention_dir(st_l, qkvg_l, log_b, True, s_b, of_l, subln, False)
    return ob_c, ob_l


def _dft_cos_sin(n):
    idx = np.arange(n)
    ang = 2.0 * np.pi * ((idx[:, None] * idx[None, :]) % n) / n
    return np.cos(ang), np.sin(ang)


def _fft_stage1_kernel(x_ref, m1_ref, twc_ref, tws_ref, o_ref, *, n1, n2b, cw):
    res = jnp.dot(m1_ref[...], x_ref[...], preferred_element_type=F32)
    for s in range(n2b):
        ar = res[:n1, s * cw:(s + 1) * cw]
        ai = res[n1:, s * cw:(s + 1) * cw]
        c = twc_ref[s]
        sn = tws_ref[s]
        o_ref[0, :, s * cw:(s + 1) * cw] = (ar * c + ai * sn).astype(o_ref.dtype)
        o_ref[1, :, s * cw:(s + 1) * cw] = (ai * c - ar * sn).astype(o_ref.dtype)


def _fft_final_kernel(z_ref, m2_ref, cc_ref, sc_ref, o_ref, *, k1b, n2, cw, complex_in):
    for kk in range(k1b):
        if complex_in:
            z = z_ref[:, kk].reshape(2 * n2, cw)
        else:
            z = z_ref[...]
        x = jnp.dot(m2_ref[...], z, preferred_element_type=F32)
        xr = x[:n2].astype(BF16)
        xi = x[n2:].astype(BF16)
        for g in range(C_GROUPS):
            lo = g * C_GROUP_DIM
            y = jnp.dot(xr[:, lo:lo + C_GROUP_DIM], cc_ref[...], preferred_element_type=F32)
            y += jnp.dot(xi[:, lo:lo + C_GROUP_DIM], sc_ref[...], preferred_element_type=F32)
            o_ref[:, kk * cw + lo:kk * cw + lo + C_GROUP_DIM] = y.astype(o_ref.dtype)


def _fourier(st, u):
    batch, length = st.batch, st.length
    cw = C_W
    n2 = 128 if (length % 128 == 0 and length > 256) else length
    n1 = length // n2
    cc, sc = _dft_cos_sin(C_GROUP_DIM)
    cc = jnp.asarray(cc / math.sqrt(C_GROUP_DIM), BF16)
    sc = jnp.asarray(sc / math.sqrt(C_GROUP_DIM), BF16)
    c2, s2 = _dft_cos_sin(n2)
    if n1 == 1:
        m2 = jnp.asarray(np.concatenate([c2, -s2], axis=0) / math.sqrt(n2), BF16)
        z = u.reshape(batch, n2, cw)
        z_spec = pl.BlockSpec((None, n2, cw), lambda b, i: (b, 0, 0))
        k1b = 1
    else:
        c1, s1 = _dft_cos_sin(n1)
        m1 = jnp.asarray(np.concatenate([c1, -s1], axis=0) / math.sqrt(n1), BF16)
        i2 = np.arange(n2)[:, None]
        k1 = np.arange(n1)[None, :]
        ang = 2.0 * np.pi * ((i2 * k1) % length) / length
        twc = jnp.asarray(np.cos(ang)[:, :, None], F32)
        tws = jnp.asarray(np.sin(ang)[:, :, None], F32)
        n2b = _pick(n2, (4, 2, 1))
        stage1 = pl.pallas_call(
            functools.partial(_fft_stage1_kernel, n1=n1, n2b=n2b, cw=cw),
            grid=(batch, n2 // n2b),
            in_specs=[
                pl.BlockSpec((None, n1, n2b * cw), lambda b, j: (b, 0, j)),
                pl.BlockSpec((2 * n1, n1), lambda b, j: (0, 0)),
                pl.BlockSpec((n2b, n1, 1), lambda b, j: (j, 0, 0)),
                pl.BlockSpec((n2b, n1, 1), lambda b, j: (j, 0, 0)),
            ],
            out_specs=pl.BlockSpec((None, 2, n1, n2b * cw), lambda b, j: (b, 0, 0, j)),
            out_shape=jax.ShapeDtypeStruct((batch, 2, n1, n2 * cw), BF16),
            compiler_params=_cparams("parallel", "parallel"),
            name="fft_stage1",
        )(u.reshape(batch, n1, n2 * cw), m1, twc, tws)
        m2 = jnp.asarray(np.block([[c2, s2], [-s2, c2]]) / math.sqrt(n2), BF16)
        z = stage1.reshape(batch, 2, n1, n2, cw)
        k1b = _pick(n1, (4, 2, 1))
        z_spec = pl.BlockSpec((None, 2, k1b, n2, cw), lambda b, i: (b, 0, i, 0, 0))
    out = pl.pallas_call(
        functools.partial(_fft_final_kernel, k1b=k1b, n2=n2, cw=cw, complex_in=n1 > 1),
        grid=(batch, n1 // k1b),
        in_specs=[
            z_spec,
            pl.BlockSpec(m2.shape, lambda b, i: (0, 0)),
            pl.BlockSpec(cc.shape, lambda b, i: (0, 0)),
            pl.BlockSpec(sc.shape, lambda b, i: (0, 0)),
        ],
        out_specs=pl.BlockSpec((None, n2, k1b * cw), lambda b, i: (b, 0, i)),
        out_shape=jax.ShapeDtypeStruct((batch, n2, n1 * cw), BF16),
        compiler_params=_cparams("parallel", "parallel"),
        name="fft_final",
    )(z, m2, cc, sc)
    return out.reshape(batch * length, cw)


def _ffn_glu_kernel(a_ref, ap_ref, an_ref, wg_ref, wv_ref, cwg_ref, cwv_ref, cbg_ref, cbv_ref, o_ref, ax_ref,
                    *, tm, hb, seq_tiles):
    i = pl.program_id(0)

    @pl.when(pl.program_id(1) == 0)
    def _():
        keep_prev = jnp.where((i % seq_tiles) == 0, 0.0, 1.0).astype(F32)
        keep_next = jnp.where((i % seq_tiles) == seq_tiles - 1, 0.0, 1.0).astype(F32)
        ax_ref[0:hb, :] = (ap_ref[...].astype(F32) * keep_prev).astype(ax_ref.dtype)
        ax_ref[hb:hb + tm, :] = a_ref[...]
        ax_ref[hb + tm:, :] = (an_ref[...].astype(F32) * keep_next).astype(ax_ref.dtype)

    ax = ax_ref[...]
    rows = tm + 2 * hb

    def branch(w_ref, cw_ref, cb_ref):
        u = jnp.dot(ax, w_ref[...], preferred_element_type=F32)
        up = pltpu.roll(u, 1, 0)[hb:hb + tm]
        dn = pltpu.roll(u, rows - 1, 0)[hb:hb + tm]
        return up * cw_ref[0:1, :] + u[hb:hb + tm] * cw_ref[1:2, :] + dn * cw_ref[2:3, :] + cb_ref[...]

    g = branch(wg_ref, cwg_ref, cbg_ref)
    v = branch(wv_ref, cwv_ref, cbv_ref)
    o_ref[...] = (g * _sigmoid(g) * v).astype(o_ref.dtype)


def _ffn_glu(st, a, w_up, conv_w, conv_b):
    m, k = a.shape
    f2 = w_up.shape[1]
    f = f2 // 2
    tm = _pick(st.length, (1024, 512, 256, 128))
    tn = _pick(f, (512, 256, 128))
    nj = f // tn
    hb = V7X_BF16_SUBLANES
    nhb = m // hb
    rpt = tm // hb
    prev = lambda i: jnp.maximum(i * rpt - 1, 0)
    nxt = lambda i: jnp.minimum((i + 1) * rpt, nhb - 1)
    return pl.pallas_call(
        functools.partial(_ffn_glu_kernel, tm=tm, hb=hb, seq_tiles=st.length // tm),
        grid=(m // tm, nj),
        in_specs=[
            pl.BlockSpec((tm, k), lambda i, j: (i, 0)),
            pl.BlockSpec((hb, k), lambda i, j: (prev(i), 0)),
            pl.BlockSpec((hb, k), lambda i, j: (nxt(i), 0)),
            pl.BlockSpec((k, tn), lambda i, j: (0, j)),
            pl.BlockSpec((k, tn), lambda i, j: (0, nj + j)),
            pl.BlockSpec((3, tn), lambda i, j: (0, j)),
            pl.BlockSpec((3, tn), lambda i, j: (0, nj + j)),
            pl.BlockSpec((1, tn), lambda i, j: (0, j)),
            pl.BlockSpec((1, tn), lambda i, j: (0, nj + j)),
        ],
        out_specs=pl.BlockSpec((tm, tn), lambda i, j: (i, j)),
        out_shape=jax.ShapeDtypeStruct((m, f), BF16),
        scratch_shapes=[pltpu.VMEM((tm + 2 * hb, k), BF16)],
        compiler_params=_cparams("parallel", "arbitrary"),
        name="ffn_up_conv_glu",
    )(a, a, a, w_up, w_up, conv_w, conv_w, conv_b.reshape(1, f2), conv_b.reshape(1, f2))


def _rope_tables(n_tokens):
    rows = n_tokens // GRID_W
    r, col = jnp.meshgrid(jnp.arange(rows), jnp.arange(GRID_W), indexing="ij")
    pos = jnp.stack([r.reshape(-1), col.reshape(-1)], axis=-1).astype(F32)
    n_freq = A_DIM // 4
    inv_freq = ROPE_BASE ** (-jnp.arange(n_freq, dtype=F32) / n_freq)
    ang = pos[:, :, None, None] * inv_freq
    ang = jnp.broadcast_to(ang, (n_tokens, 2, 2, n_freq)).reshape(n_tokens, A_DIM)
    sign = jnp.where((jnp.arange(A_DIM) % (A_DIM // 2)) < (A_DIM // 4), -1.0, 1.0).astype(F32)
    return jnp.cos(ang), jnp.sin(ang) * sign


def _mixer_and_ffn(st_l, st_c, cos, sin_s, x_l, x_c, wts):
    mods = wts["mods"]
    w_in = wts["w_in"].astype(BF16)
    d = x_l.shape[1]
    streams = ((st_l, x_l), (st_c, x_c))

    sections = _proj_sections(d)
    proj = []
    for st, x in streams:
        h = _norm_mod(st, x, wts["norm1_g"], mods, 0)
        proj.append({name: _proj(st, h, w_in, sec, cos, sin_s) for name, sec in sections.items()})
    pl_, pc_ = proj

    oa_l = _attention(st_l, pl_["qa"], pc_["ka"], pc_["va"], pl_["ka"], pl_["va"], wts["scal"], wts["attn_subln"])
    oa_c = _attention(st_c, pc_["qa"], pc_["ka"], pc_["va"], None, None, wts["scal"], wts["attn_subln"])
    ob_c, ob_l = _retention(st_c, st_l, tuple(pc_[n] for n in ("qb", "kb", "vb", "gb")),
                            tuple(pl_[n] for n in ("qb", "kb", "vb", "gb")),
                            wts["log_f"], wts["log_b"], wts["ret_subln"])
    oc_l = _fourier(st_l, pl_["uc"])
    oc_c = _fourier(st_c, pc_["uc"])
    pg_l, pg_c = pl_["gates"], pc_["gates"]

    w_pa = wts["w_pa"].astype(BF16)
    w_pb = wts["w_pb"].astype(BF16)
    w_pc = wts["w_pc"].astype(BF16)
    w_out = wts["w_out"].astype(BF16)
    w_up = wts["w_up"].astype(BF16)
    w_down = wts["w_down"].astype(BF16)

    outs = []
    for st, x, oa, ob, oc, pg in ((st_l, x_l, oa_l, ob_l, oc_l, pg_l), (st_c, x_c, oa_c, ob_c, oc_c, pg_c)):
        y = _merge(st, oa, ob, oc, w_pa, w_pb, w_pc, pg)
        x1 = _res_matmul(st, y, w_out, x, mods, 2)
        h2 = _norm_mod(st, x1, wts["norm2_g"], mods, 3)
        act = _ffn_glu(st, h2, w_up, wts["conv_w"], wts["conv_b"])
        outs.append(_res_matmul(st, act, w_down, x1, mods, 5))
    return outs[0], outs[1]


def kernel(x, c, ctx, c_ctx, w_ada_down, w_ada_up, b_ada, norm1_g, norm2_g, w_in, lam_q1, lam_k1, lam_q2,
           lam_k2, attn_subln, ret_decay_fwd, ret_decay_bwd, ret_subln, w_pa, w_pb, w_pc, w_out, w_up,
           conv_w, conv_b, w_down, final_g):
    batch, seq, d = x.shape
    lc = ctx.shape[1]
    depth = w_in.shape[0]
    assert batch + 1 <= V7X_MOD_ROWS
    st_l = _Stream(batch, seq, False)
    st_c = _Stream(batch, lc, True)

    cond = jnp.zeros((V7X_MOD_ROWS, d), F32).at[:batch].set(c.astype(F32)).at[batch].set(c_ctx.astype(F32))
    mods = _ada_params(cond, w_ada_down, w_ada_up, b_ada)

    lam_init = jnp.asarray([0.8 - 0.6 * math.exp(-0.3 * l) for l in range(depth)], F32)
    lam = (jnp.exp(jnp.sum(lam_q1.astype(F32) * lam_k1.astype(F32), axis=-1))
           - jnp.exp(jnp.sum(lam_q2.astype(F32) * lam_k2.astype(F32), axis=-1)) + lam_init)
    scal = jnp.stack([lam, 1.0 - lam_init], axis=-1)
    log_f = jax.nn.log_sigmoid(ret_decay_fwd.astype(F32))
    log_b = jax.nn.log_sigmoid(ret_decay_bwd.astype(F32))
    cos, sin_s = _rope_tables(seq)

    xs = dict(mods=mods, scal=scal, log_f=log_f, log_b=log_b, norm1_g=norm1_g, norm2_g=norm2_g, w_in=w_in,
              attn_subln=attn_subln, ret_subln=ret_subln, w_pa=w_pa, w_pb=w_pb, w_pc=w_pc, w_out=w_out,
              w_up=w_up, conv_w=conv_w, conv_b=conv_b, w_down=w_down)

    def layer(carry, wts):
        x_l, x_c = carry
        return _mixer_and_ffn(st_l, st_c, cos, sin_s, x_l, x_c, wts), None

    (x_l, _), _ = lax.scan(layer, (x.reshape(batch * seq, d), ctx.reshape(batch * lc, d)), xs)
    return _final_norm(x_l, final_g).reshape(batch, seq, d)
```

```python
import functools
import math

import numpy as np
import jax
import jax.numpy as jnp
from jax import lax
from jax.experimental import pallas as pl
from jax.experimental.pallas import tpu as pltpu

F32 = jnp.float32
BF16 = jnp.bfloat16

GRID_W = 64
N_MOD = 6
A_HEADS = 8
A_DIM = 128
B_HEADS = 8
B_DK = 128
B_DV = 256
C_GROUPS = 4
C_GROUP_DIM = 512
ROPE_BASE = 10000.0
NORM_EPS = 1e-6
LOG2_E = math.log2(math.e)
A_QK =A_HEADS * 2 * A_DIM
A_V = A_HEADS * 2 * A_DIM
B_QK = B_HEADS * B_DK
B_V = B_HEADS * B_DV
C_W = C_GROUPS * C_GROUP_DIM
OFF_KA = 0
OFF_VA = OFF_KA + A_QK
OFF_KB = OFF_VA + A_V
OFF_VB = OFF_KB + B_QK
OFF_QA = OFF_VB + B_V
OFF_QB = OFF_QA + A_QK
OFF_GB = OFF_QB + B_QK
OFF_UC = OFF_GB + B_V
OFF_GATES = OFF_UC + C_W

V7X_LANES = 128
V7X_BF16_SUBLANES = 16
V7X_MOD_ROWS = 8
V7X_VMEM_LIMIT_BYTES = 56 * 1024 * 1024


def _cparams(*sem, flags=None):
    return pltpu.CompilerParams(dimension_semantics=sem, vmem_limit_bytes=V7X_VMEM_LIMIT_BYTES, flags=flags)


def _pick(n, prefs):
    for p in prefs:
        if n % p == 0:
            return p
    return n


def _smem():
    return pl.BlockSpec(memory_space=pltpu.SMEM)


def _layer_call(kernel, layer, *, grid, in_specs, out_specs, out_shape, scratch_shapes=(), **kwargs):
    def body(layer_ref, *refs):
        del layer_ref
        kernel(*refs)

    return functools.partial(
        pl.pallas_call(
            body,
            grid_spec=pltpu.PrefetchScalarGridSpec(num_scalar_prefetch=1, grid=grid, in_specs=in_specs,
                                                   out_specs=out_specs, scratch_shapes=list(scratch_shapes)),
            out_shape=out_shape, **kwargs),
        layer)


def _ada_kernel(cond_ref, wd_ref, wu_ref, b_ref, o_ref, t_ref):
    @pl.when((pl.program_id(1) == 0) & (pl.program_id(2) == 0))
    def _():
        cnd = cond_ref[...]
        s = cnd * jax.nn.sigmoid(cnd)
        t_ref[...] = jnp.dot(s.astype(BF16), wd_ref[...].astype(BF16), preferred_element_type=F32)

    o_ref[...] = jnp.dot(t_ref[...].astype(BF16), wu_ref[...].astype(BF16),
                         preferred_element_type=F32) + b_ref[...]


def _ada_params(cond, w_down, w_up, b):
    depth, d, r = w_down.shape
    tn = _pick(d, (2048, 1024, 512, 256, 128))
    nj = d // tn
    out = pl.pallas_call(
        _ada_kernel,
        grid=(depth, N_MOD, nj),
        in_specs=[
            pl.BlockSpec((V7X_MOD_ROWS, d), lambda l, k, j: (0, 0)),
            pl.BlockSpec((None, d, r), lambda l, k, j: (l, 0, 0)),
            pl.BlockSpec((None, r, tn), lambda l, k, j: (l, 0, k * nj + j)),
            pl.BlockSpec((None, 1, tn), lambda l, k, j: (l, 0, k * nj + j)),
        ],
        out_specs=pl.BlockSpec((None, None, V7X_MOD_ROWS, tn), lambda l, k, j: (l, k, 0, j)),
        out_shape=jax.ShapeDtypeStruct((depth, N_MOD, V7X_MOD_ROWS, d), F32),
        scratch_shapes=[pltpu.VMEM((V7X_MOD_ROWS, r), F32)],
        compiler_params=_cparams("arbitrary", "arbitrary", "arbitrary"),
        name="ada_params",
    )(cond, w_down, w_up, b.reshape(depth, 1, N_MOD * d))
    return out.reshape(depth, N_MOD, V7X_MOD_ROWS, 1, d)


class _Stream:
    def __init__(self, batch, length, is_ctx):
        self.batch = batch
        self.length = length
        self.is_ctx = is_ctx
        self.rows = batch * length
        self.row_span = self.rows if is_ctx else length

    def seg(self, tile_rows):
        if self.is_ctx:
            return lambda i: self.batch
        per = self.length // tile_rows
        return lambda i: i // per


def _norm_mod_kernel(x_ref, g_ref, sh_ref, sc_ref, o_ref):
    x = x_ref[...]
    y = x * lax.rsqrt(jnp.mean(x * x, axis=-1, keepdims=True) + NORM_EPS) * g_ref[...]
    o_ref[...] = (y * (1.0 + sc_ref[...]) + sh_ref[...]).astype(o_ref.dtype)


def _norm_kernel(x_ref, g_ref, o_ref):
    x = x_ref[...]
    y = x * lax.rsqrt(jnp.mean(x * x, axis=-1, keepdims=True) + NORM_EPS) * g_ref[...]
    o_ref[...] = y.astype(o_ref.dtype)


def _norm_mod(st, x, g, mods, k_shift):
    m, d = x.shape
    tr = _pick(st.row_span, (512, 256, 128, 64, 32, 16))
    seg = st.seg(tr)
    return pl.pallas_call(
        _norm_mod_kernel,
        grid=(m // tr,),
        in_specs=[
            pl.BlockSpec((tr, d), lambda i: (i, 0)),
            pl.BlockSpec((1, d), lambda i: (0, 0)),
            pl.BlockSpec((None, None, 1, d), lambda i: (k_shift, seg(i), 0, 0)),
            pl.BlockSpec((None, None, 1, d), lambda i: (k_shift + 1, seg(i), 0, 0)),
        ],
        out_specs=pl.BlockSpec((tr, d), lambda i: (i, 0)),
        out_shape=jax.ShapeDtypeStruct((m, d), BF16),
        compiler_params=_cparams("parallel"),
        name="norm_mod",
    )(x, g.reshape(1, d), mods, mods)


def _final_norm(x, g):
    m, d = x.shape
    tr = _pick(m, (512, 256, 128, 64, 32, 16, 8))
    return pl.pallas_call(
        _norm_kernel,
        grid=(m // tr,),
        in_specs=[pl.BlockSpec((tr, d), lambda i: (i, 0)), pl.BlockSpec((1, d), lambda i: (0, 0))],
        out_specs=pl.BlockSpec((tr, d), lambda i: (i, 0)),
        out_shape=jax.ShapeDtypeStruct((m, d), F32),
        compiler_params=_cparams("parallel"),
        name="final_norm",
    )(x, g.reshape(1, d))


def _sigmoid(x):
    return 0.5 * jnp.tanh(0.5 * x) + 0.5


def _proj_kernel(*refs, mode, scale):
    if mode == "rope":
        h_ref, w_ref, cos_ref, sin_ref, o_ref = refs
    else:
        h_ref, w_ref, o_ref = refs
    acc = jnp.dot(h_ref[...], w_ref[...], preferred_element_type=F32)
    if mode == "rope":
        cos = cos_ref[...] * scale
        sin = sin_ref[...] * scale
        lane = lax.broadcasted_iota(jnp.int32, cos.shape, 1)
        first_half = (lane % (A_DIM // 2)) < (A_DIM // 4)
        for c in range(acc.shape[1] // A_DIM):
            xc = acc[:, c * A_DIM:(c + 1) * A_DIM]
            rot = jnp.where(first_half, pltpu.roll(xc, A_DIM - A_DIM // 4, 1), pltpu.roll(xc, A_DIM // 4, 1))
            o_ref[:, c * A_DIM:(c + 1) * A_DIM] = (xc * cos + rot * sin).astype(o_ref.dtype)
    elif mode == "silu":
        o_ref[...] = (acc * _sigmoid(acc)).astype(o_ref.dtype)
    elif mode == "sigmoid":
        o_ref[...] = _sigmoid(acc).astype(o_ref.dtype)
    elif scale != 1.0:
        o_ref[...] = (acc * scale).astype(o_ref.dtype)
    else:
        o_ref[...] = acc.astype(o_ref.dtype)


def _proj_sections(d):
    return dict(
        ka=(OFF_KA, A_QK, "rope", 1.0), va=(OFF_VA, A_V, "plain", 1.0),
        kb=(OFF_KB, B_QK, "rope", B_DK ** -0.5), vb=(OFF_VB, B_V, "plain", 1.0),
        qa=(OFF_QA, A_QK, "rope", A_DIM ** -0.5 * LOG2_E), qb=(OFF_QB, B_QK, "rope", 1.0),
        gb=(OFF_GB, B_V, "silu", 1.0), uc=(OFF_UC, C_W, "plain", 1.0),
        gates=(OFF_GATES, 3 * d, "sigmoid", 1.0))


def _proj(st, layer, h, w, section, cos, sin_signed):
    col0, ncols, mode, scale = section
    if mode == "rope" and st.is_ctx:
        mode = "plain"
    m, k = h.shape
    tm = _pick(st.row_span, (1024, 512, 256, 128))
    tn = _pick(ncols, (1024, 512, 256, 128))
    per = st.length // tm
    j0 = col0 // tn
    in_specs = [pl.BlockSpec((tm, k), lambda j, i, l: (i, 0)),
                pl.BlockSpec((None, k, tn), lambda j, i, l: (l[0], 0, j0 + j))]
    args = [h, w]
    if mode == "rope":
        in_specs += [pl.BlockSpec((tm, A_DIM), lambda j, i, l: (i % per, 0))] * 2
        args += [cos, sin_signed]
    return _layer_call(
        functools.partial(_proj_kernel, mode=mode, scale=scale), layer,
        grid=(ncols // tn, m // tm),
        in_specs=in_specs,
        out_specs=pl.BlockSpec((tm, tn), lambda j, i, l: (i, j)),
        out_shape=jax.ShapeDtypeStruct((m, ncols), BF16),
        compiler_params=_cparams("parallel", "parallel"),
        name="in_proj_" + mode,
    )(*args)


def _res_mm_kernel(a_ref, w_ref, x_ref, g_ref, o_ref):
    y = jnp.dot(a_ref[...], w_ref[...], preferred_element_type=F32)
    o_ref[...] = x_ref[...] + g_ref[...] * y


def _res_matmul(st, layer, a, w, x, mods, k_gate):
    m, k = a.shape
    n = w.shape[2]
    tm = _pick(st.row_span, (1024, 512, 256, 128))
    tn = _pick(n, (512, 256, 128))
    seg = st.seg(tm)
    return _layer_call(
        _res_mm_kernel, layer,
        grid=(m // tm, n // tn),
        in_specs=[
            pl.BlockSpec((tm, k), lambda i, j, l: (i, 0)),
            pl.BlockSpec((None, k, tn), lambda i, j, l: (l[0], 0, j)),
            pl.BlockSpec((tm, tn), lambda i, j, l: (i, j)),
            pl.BlockSpec((None, None, 1, tn), lambda i, j, l: (k_gate, seg(i), 0, j)),
        ],
        out_specs=pl.BlockSpec((tm, tn), lambda i, j, l: (i, j)),
        out_shape=jax.ShapeDtypeStruct((m, n), F32),
        compiler_params=_cparams("parallel", "parallel"),
        name="res_matmul",
    )(a, w, x, mods)


def _merge_kernel(oa_ref, ob_ref, oc_ref, wa_ref, wb_ref, wc_ref, ga_ref, gb_ref, gc_ref, o_ref):
    y = ga_ref[...].astype(F32) * jnp.dot(oa_ref[...], wa_ref[...], preferred_element_type=F32)
    y += gb_ref[...].astype(F32) * jnp.dot(ob_ref[...], wb_ref[...], preferred_element_type=F32)
    y += gc_ref[...].astype(F32) * jnp.dot(oc_ref[...], wc_ref[...], preferred_element_type=F32)
    o_ref[...] = y.astype(o_ref.dtype)


def _merge(st, layer, oa, ob, oc, w_pa, w_pb, w_pc, gates):
    m = oa.shape[0]
    d = w_pa.shape[2]
    tm = _pick(st.row_span, (1024, 512, 256, 128))
    tn = _pick(d, (512, 256, 128))
    nj = d // tn
    a_spec = lambda kk: pl.BlockSpec((tm, kk), lambda i, j, l: (i, 0))
    w_spec = lambda kk: pl.BlockSpec((None, kk, tn), lambda i, j, l: (l[0], 0, j))
    g_spec = lambda br: pl.BlockSpec((tm, tn), lambda i, j, l: (i, br * nj + j))
    return _layer_call(
        _merge_kernel, layer,
        grid=(m // tm, nj),
        in_specs=[a_spec(A_V), a_spec(B_V), a_spec(C_W), w_spec(A_V), w_spec(B_V), w_spec(C_W),
                  g_spec(0), g_spec(1), g_spec(2)],
        out_specs=pl.BlockSpec((tm, tn), lambda i, j, l: (i, j)),
        out_shape=jax.ShapeDtypeStruct((m, d), BF16),
        compiler_params=_cparams("parallel", "parallel"),
        name="merge",
    )(oa, ob, oc, w_pa, w_pb, w_pc, gates, gates, gates)


def _attn_kernel(*refs, tq, tk, n_lat):
    if n_lat:
        (scal_ref, q_ref, kc_ref, vc_ref, kl_ref, vl_ref, sub_ref, o_ref, m_ref, l_ref, acc_ref,
         s_a, s_b, s_c, p_a, p_b, p_c, al_a, al_b, al_c) = refs
    else:
        scal_ref, q_ref, kc_ref, vc_ref, sub_ref, o_ref, m_ref, l_ref, acc_ref = refs
    q = q_ref[...]
    m_ref[...] = jnp.full(m_ref.shape, -1e30, F32)
    l_ref[...] = jnp.zeros(l_ref.shape, F32)
    acc_ref[...] = jnp.zeros(acc_ref.shape, F32)
    nt_dims = (((1,), (1,)), ((), ()))

    def qk(k):
        return jnp.concatenate(
            [lax.dot_general(q[:, :A_DIM], k[:, :A_DIM], nt_dims, preferred_element_type=F32),
             lax.dot_general(q[:, A_DIM:], k[:, A_DIM:], nt_dims, preferred_element_type=F32)], axis=0)

    def softmax(s):
        m_prev = m_ref[...]
        m_next = jnp.maximum(m_prev, jnp.max(s, axis=1, keepdims=True))
        alpha = jnp.exp2(m_prev - m_next)
        tiles = [jnp.exp2(s[:, t * V7X_LANES:(t + 1) * V7X_LANES] - m_next)
                 for t in range(s.shape[1] // V7X_LANES)]
        l_ref[...] = alpha * l_ref[...] + functools.reduce(lambda a, b: a + b, tiles)
        m_ref[...] = m_next
        return jnp.concatenate(tiles, axis=1).astype(BF16), alpha

    def pv(p, alpha, v):
        acc_ref[...] = (jnp.concatenate([alpha, alpha], axis=1) * acc_ref[...]
                        + jnp.dot(p, v, preferred_element_type=F32))

    if not n_lat:
        p, alpha = softmax(qk(kc_ref[...]))
        pv(p, alpha, vc_ref[...])
    else:
        def kchunk(c):
            return kl_ref[pl.ds(pl.multiple_of(c * tk, tk), tk), :]

        def vchunk(c):
            return vl_ref[pl.ds(pl.multiple_of(c * tk, tk), tk), :]

        def step(k_next, s_next, s_cur, p_cur, al_cur, p_prev, al_prev, v_prev):
            if s_next is not None:
                s_next[...] = qk(k_next)
            p_cur[...], al_cur[...] = softmax(s_cur[...])
            if p_prev is not None:
                pv(p_prev[...], al_prev[...], v_prev)

        s_c[...] = qk(kc_ref[...])
        step(kchunk(0), s_a, s_c, p_c, al_c, None, None, None)
        step(kchunk(1), s_b, s_a, p_a, al_a, p_c, al_c, vc_ref[...])

        def body(j, carry):
            step(kchunk(2 * j + 2), s_a, s_b, p_b, al_b, p_a, al_a, vchunk(2 * j))
            step(kchunk(2 * j + 3), s_b, s_a, p_a, al_a, p_b, al_b, vchunk(2 * j + 1))
            return carry

        lax.fori_loop(0, (n_lat - 2) // 2, body, 0, unroll=True)
        step(None, None, s_b, p_b, al_b, p_a, al_a, vchunk(n_lat - 2))
        pv(p_b[...], al_b[...], vchunk(n_lat - 1))

    lam = scal_ref[0]
    post = scal_ref[1]
    inv_l = 1.0 / jnp.sum(l_ref[...], axis=1, keepdims=True)
    o = acc_ref[:tq] * inv_l[:tq] - lam * (acc_ref[tq:] * inv_l[tq:])
    y = o * lax.rsqrt(jnp.mean(o * o, axis=-1, keepdims=True) + NORM_EPS) * sub_ref[...] * post
    o_ref[...] = y.astype(o_ref.dtype)


def _attention(st_q, q, k_ctx, v_ctx, k_lat, v_lat, scal, subln):
    batch = st_q.batch
    hw = 2 * A_DIM
    lc = k_ctx.shape[0] // batch
    tq = _pick(st_q.length, (512, 256, 128))
    nq = st_q.length // tq
    with_lat = k_lat is not None
    ll = k_lat.shape[0] // batch if with_lat else 0
    tk = _pick(ll, (512, 256, 128)) if with_lat else 0
    n_lat = ll // tk if with_lat else 0
    in_specs = [
        _smem(),
        pl.BlockSpec((tq, hw), lambda b, h, i: (b * nq + i, h)),
        pl.BlockSpec((lc, hw), lambda b, h, i: (b, h)),
        pl.BlockSpec((lc, hw), lambda b, h, i: (b, h)),
    ]
    args = [scal, q, k_ctx, v_ctx]
    if with_lat:
        in_specs += [pl.BlockSpec((ll, hw), lambda b, h, i: (b, h))] * 2
        args += [k_lat, v_lat]
    in_specs.append(pl.BlockSpec((1, hw), lambda b, h, i: (0, 0)))
    args.append(subln.reshape(1, hw))
    scratch = [pltpu.VMEM((2 * tq, V7X_LANES), F32), pltpu.VMEM((2 * tq, V7X_LANES), F32),
               pltpu.VMEM((2 * tq, hw), F32)]
    if with_lat:
        assert n_lat >= 2 and n_lat % 2 == 0, "latent key chunks are pipelined in pairs"
        scratch += [pltpu.VMEM((2 * tq, tk), F32)] * 2 + [pltpu.VMEM((2 * tq, lc), F32)]
        scratch += [pltpu.VMEM((2 * tq, tk), BF16)] * 2 + [pltpu.VMEM((2 * tq, lc), BF16)]
        scratch += [pltpu.VMEM((2 * tq, V7X_LANES), F32)] * 3
    return pl.pallas_call(
        functools.partial(_attn_kernel, tq=tq, tk=tk, n_lat=n_lat),
        grid=(batch, A_HEADS, nq),
        in_specs=in_specs,
        out_specs=pl.BlockSpec((tq, hw), lambda b, h, i: (b * nq + i, h)),
        out_shape=jax.ShapeDtypeStruct((st_q.rows, A_V), BF16),
        scratch_shapes=scratch,
        compiler_params=_cparams("parallel", "parallel", "parallel"),
        name="diff_attn",
    )(*args)


def _ret_kernel(*refs, bwd, tb, has_init, finalize, want_state):
    refs = list(refs)
    lg_ref, q_ref, k_ref, v_ref = refs[:4]
    pos_ = 4
    s0_ref = None
    if has_init:
        s0_ref = refs[pos_]
        pos_ += 1
    if finalize:
        oprev_ref, gate_ref, sub_ref = refs[pos_:pos_ + 3]
        pos_ += 3
    o_ref = refs[pos_]
    pos_ += 1
    sfin_ref = None
    if want_state:
        sfin_ref = refs[pos_]
        pos_ += 1
    s_ref, d_ref = refs[pos_:pos_ + 2]

    t = pl.program_id(1)

    @pl.when(t == 0)
    def _():
        if has_init:
            s_ref[...] = s0_ref[...]
        else:
            s_ref[...] = jnp.zeros(s_ref.shape, F32)
        ii = lax.broadcasted_iota(jnp.int32, (tb, tb), 0)
        jj = lax.broadcasted_iota(jnp.int32, (tb, tb), 1)
        rel = ((jj - ii) if bwd else (ii - jj)).astype(F32)
        for h in range(B_HEADS):
            d_ref[h] = jnp.where(rel >= 0, jnp.exp(lg_ref[h] * jnp.maximum(rel, 0.0)), 0.0)

    pos = lax.broadcasted_iota(jnp.int32, (tb, 1), 0).astype(F32)
    q_exp = (tb - pos) if bwd else (pos + 1.0)
    k_exp = pos if bwd else (tb - 1.0 - pos)
    for h in range(B_HEADS):
        lg = lg_ref[h]
        q = q_ref[:, h * B_DK:(h + 1) * B_DK]
        k = k_ref[:, h * B_DK:(h + 1) * B_DK]
        v = v_ref[:, h * B_DV:(h + 1) * B_DV]
        scores = lax.dot_general(q, k, (((1,), (1,)), ((), ())), preferred_element_type=F32) * d_ref[h]
        intra = jnp.dot(scores.astype(BF16), v, preferred_element_type=F32)
        state = s_ref[h]
        q_dec = (q.astype(F32) * jnp.exp(lg * q_exp)).astype(BF16)
        out = intra + jnp.dot(q_dec, state.astype(BF16), preferred_element_type=F32)
        k_dec = (k.astype(F32) * jnp.exp(lg * k_exp)).astype(BF16)
        kv = lax.dot_general(k_dec, v, (((0,), (0,)), ((), ())), preferred_element_type=F32)
        s_ref[h] = jnp.exp(jnp.full((1, 1), lg * tb, F32)) * state + kv
        if finalize:
            tot = out + oprev_ref[:, h * B_DV:(h + 1) * B_DV].astype(F32)
            y = tot * lax.rsqrt(jnp.mean(tot * tot, axis=-1, keepdims=True) + NORM_EPS) * sub_ref[...]
            o_ref[:, h * B_DV:(h + 1) * B_DV] = (
                y * gate_ref[:, h * B_DV:(h + 1) * B_DV].astype(F32)).astype(o_ref.dtype)
        else:
            o_ref[:, h * B_DV:(h + 1) * B_DV] = out.astype(o_ref.dtype)

    if want_state:
        @pl.when(t == pl.num_programs(1) - 1)
        def _():
            sfin_ref[...] = s_ref[...]


def _retention_dir(st, qkvg, log_g, bwd, s0, oprev, subln, want_state):
    q, k, v, gate = qkvg
    batch = st.batch
    tb = _pick(st.length, (256, 128))
    nt = st.length // tb
    finalize = oprev is not None
    has_init = s0 is not None

    if bwd:
        rb = lambda b, t: b * nt + (nt - 1 - t)
    else:
        rb = lambda b, t: b * nt + t

    row_blk = lambda width: pl.BlockSpec((tb, width), lambda b, t: (rb(b, t), 0))
    state_blk = pl.BlockSpec((None, B_HEADS, B_DK, B_DV), lambda b, t: (b, 0, 0, 0))
    in_specs = [_smem(), row_blk(B_QK), row_blk(B_QK), row_blk(B_V)]
    args = [log_g, q, k, v]
    if has_init:
        in_specs.append(state_blk)
        args.append(s0)
    if finalize:
        in_specs += [row_blk(B_V), row_blk(B_V), pl.BlockSpec((1, B_DV), lambda b, t: (0, 0))]
        args += [oprev, gate, subln.reshape(1, B_DV)]
    out_specs = [row_blk(B_V)]
    out_shape = [jax.ShapeDtypeStruct((st.rows, B_V), BF16)]
    if want_state:
        out_specs.append(state_blk)
        out_shape.append(jax.ShapeDtypeStruct((batch, B_HEADS, B_DK, B_DV), F32))
    res = pl.pallas_call(
        functools.partial(_ret_kernel, bwd=bwd, tb=tb, has_init=has_init, finalize=finalize,
                          want_state=want_state),
        grid=(batch, nt),
        in_specs=in_specs,
        out_specs=out_specs,
        out_shape=out_shape,
        scratch_shapes=[pltpu.VMEM((B_HEADS, B_DK, B_DV), F32), pltpu.VMEM((B_HEADS, tb, tb), F32)],
        compiler_params=_cparams("parallel", "arbitrary"),
        name="retention_bwd" if bwd else "retention_fwd",
    )(*args)
    return (res[0], res[1]) if want_state else (res[0], None)


def _retention(st_c, st_l, qkvg_c, qkvg_l, log_f, log_b, subln):
    of_c, s_f = _retention_dir(st_c, qkvg_c, log_f, False, None, None, None, True)
    ob_c, s_b = _retention_dir(st_c, qkvg_c, log_b, True, None, of_c, subln, True)
    of_l, _ = _retention_dir(st_l, qkvg_l, log_f, False, s_f, None, None, False)
    ob_l, _ = _retention_dir(st_l, qkvg_l, log_b, True, s_b, of_l, subln, False)
    return ob_c, ob_l


def _dft_cos_sin(n):
    idx = np.arange(n)
    ang = 2.0 * np.pi * ((idx[:, None] * idx[None, :]) % n) / n
    return np.cos(ang), np.sin(ang)


def _fft_stage1_kernel(x_ref, m1_ref, twc_ref, tws_ref, o_ref, *, n1, n2b, cw):
    for s in range(n2b):
        res = jnp.dot(m1_ref[...], x_ref[s], preferred_element_type=F32)
        ar = res[:n1]
        ai = res[n1:]
        c = twc_ref[s]
        sn = tws_ref[s]
        o_ref[0, :, s * cw:(s + 1) * cw] = (ar * c + ai * sn).astype(o_ref.dtype)
        o_ref[1, :, s * cw:(s + 1) * cw] = (ai * c - ar * sn).astype(o_ref.dtype)


def _fft_final_kernel(z_ref, m2_ref, cc_ref, sc_ref, o_ref, *, k1b, n2, cw, complex_in):
    for kk in range(k1b):
        if complex_in:
            z = z_ref[:, kk].reshape(2 * n2, cw)
        else:
            z = z_ref[...]
        x = jnp.dot(m2_ref[...], z, preferred_element_type=F32)
        xr = x[:n2].astype(BF16)
        xi = x[n2:].astype(BF16)
        for g in range(C_GROUPS):
            lo = g * C_GROUP_DIM
            y = jnp.dot(xr[:, lo:lo + C_GROUP_DIM], cc_ref[...], preferred_element_type=F32)
            y += jnp.dot(xi[:, lo:lo + C_GROUP_DIM], sc_ref[...], preferred_element_type=F32)
            o_ref[kk, :, lo:lo + C_GROUP_DIM] = y.astype(o_ref.dtype)


def _fourier(st, u):
    batch, length = st.batch, st.length
    cw = C_W
    n2 = 128 if (length % 128 == 0 and length > 256) else length
    n1 = length // n2
    cc, sc = _dft_cos_sin(C_GROUP_DIM)
    cc = jnp.asarray(cc / math.sqrt(C_GROUP_DIM), BF16)
    sc = jnp.asarray(sc / math.sqrt(C_GROUP_DIM), BF16)
    c2, s2 = _dft_cos_sin(n2)
    if n1 == 1:
        m2 = jnp.asarray(np.concatenate([c2, -s2], axis=0) / math.sqrt(n2), BF16)
        z = u.reshape(batch, n2, cw)
        z_spec = pl.BlockSpec((None, n2, cw), lambda b, i: (b, 0, 0))
        k1b = 1
    else:
        c1, s1 = _dft_cos_sin(n1)
        m1 = jnp.asarray(np.concatenate([c1, -s1], axis=0) / math.sqrt(n1), BF16)
        i2 = np.arange(n2)[:, None]
        k1 = np.arange(n1)[None, :]
        ang = 2.0 * np.pi * ((i2 * k1) % length) / length
        twc = jnp.asarray(np.cos(ang)[:, :, None], F32)
        tws = jnp.asarray(np.sin(ang)[:, :, None], F32)
        n2b = _pick(n2, (8, 4, 2, 1))
        u_t = jnp.swapaxes(u.reshape(batch, n1, n2, cw), 1, 2)
        stage1 = pl.pallas_call(
            functools.partial(_fft_stage1_kernel, n1=n1, n2b=n2b, cw=cw),
            grid=(batch, n2 // n2b),
            in_specs=[
                pl.BlockSpec((None, n2b, n1, cw), lambda b, j: (b, j, 0, 0)),
                pl.BlockSpec((2 * n1, n1), lambda b, j: (0, 0)),
                pl.BlockSpec((n2b, n1, 1), lambda b, j: (j, 0, 0)),
                pl.BlockSpec((n2b, n1, 1), lambda b, j: (j, 0, 0)),
            ],
            out_specs=pl.BlockSpec((None, 2, n1, n2b * cw), lambda b, j: (b, 0, 0, j)),
            out_shape=jax.ShapeDtypeStruct((batch, 2, n1, n2 * cw), BF16),
            compiler_params=_cparams("parallel", "parallel"),
            name="fft_stage1",
        )(u_t, m1, twc, tws)
        m2 = jnp.asarray(np.block([[c2, s2], [-s2, c2]]) / math.sqrt(n2), BF16)
        z = stage1.reshape(batch, 2, n1, n2, cw)
        k1b = _pick(n1, (4, 2, 1))
        z_spec = pl.BlockSpec((None, 2, k1b, n2, cw), lambda b, i: (b, 0, i, 0, 0))
    out = pl.pallas_call(
        functools.partial(_fft_final_kernel, k1b=k1b, n2=n2, cw=cw, complex_in=n1 > 1),
        grid=(batch, n1 // k1b),
        in_specs=[
            z_spec,
            pl.BlockSpec(m2.shape, lambda b, i: (0, 0)),
            pl.BlockSpec(cc.shape, lambda b, i: (0, 0)),
            pl.BlockSpec(sc.shape, lambda b, i: (0, 0)),
        ],
        out_specs=pl.BlockSpec((None, k1b, n2, cw), lambda b, i: (b, i, 0, 0)),
        out_shape=jax.ShapeDtypeStruct((batch, n1, n2, cw), BF16),
        compiler_params=_cparams("parallel", "parallel"),
        name="fft_final",
    )(z, m2, cc, sc)
    return jnp.swapaxes(out, 1, 2).reshape(batch * length, cw)


def _ffn_glu_kernel(a_ref, ap_ref, an_ref, wg_ref, wv_ref, cwg_ref, cwv_ref, cbg_ref, cbv_ref, o_ref, ax_ref,
                    *, tm, hb, seq_tiles):
    i = pl.program_id(0)

    @pl.when(pl.program_id(1) == 0)
    def _():
        keep_prev = jnp.where((i % seq_tiles) == 0, 0.0, 1.0).astype(F32)
        keep_next = jnp.where((i % seq_tiles) == seq_tiles - 1, 0.0, 1.0).astype(F32)
        ax_ref[0:hb, :] = (ap_ref[...].astype(F32) * keep_prev).astype(ax_ref.dtype)
        ax_ref[hb:hb + tm, :] = a_ref[...]
        ax_ref[hb + tm:, :] = (an_ref[...].astype(F32) * keep_next).astype(ax_ref.dtype)

    ax = ax_ref[...]
    rows = tm + 2 * hb

    def branch(w_ref, cw_ref, cb_ref):
        u = jnp.dot(ax, w_ref[...], preferred_element_type=F32)
        up = pltpu.roll(u, 1, 0)[hb:hb + tm]
        dn = pltpu.roll(u, rows - 1, 0)[hb:hb + tm]
        return up * cw_ref[0:1, :] + u[hb:hb + tm] * cw_ref[1:2, :] + dn * cw_ref[2:3, :] + cb_ref[...]

    g = branch(wg_ref, cwg_ref, cbg_ref)
    v = branch(wv_ref, cwv_ref, cbv_ref)
    o_ref[...] = (g * _sigmoid(g) * v).astype(o_ref.dtype)


def _ffn_glu(st, layer, a, w_up, conv_w, conv_b):
    m, k = a.shape
    f2 = w_up.shape[2]
    f = f2 // 2
    tm = _pick(st.length, (1024, 512, 256, 128))
    tn = _pick(f, (512, 256, 128))
    nj = f // tn
    hb = V7X_BF16_SUBLANES
    nhb = m // hb
    rpt = tm // hb
    prev = lambda i: jnp.maximum(i * rpt - 1, 0)
    nxt = lambda i: jnp.minimum((i + 1) * rpt, nhb - 1)
    return _layer_call(
        functools.partial(_ffn_glu_kernel, tm=tm, hb=hb, seq_tiles=st.length // tm), layer,
        grid=(m // tm, nj),
        in_specs=[
            pl.BlockSpec((tm, k), lambda i, j, l: (i, 0)),
            pl.BlockSpec((hb, k), lambda i, j, l: (prev(i), 0)),
            pl.BlockSpec((hb, k), lambda i, j, l: (nxt(i), 0)),
            pl.BlockSpec((None, k, tn), lambda i, j, l: (l[0], 0, j)),
            pl.BlockSpec((None, k, tn), lambda i, j, l: (l[0], 0, nj + j)),
            pl.BlockSpec((3, tn), lambda i, j, l: (0, j)),
            pl.BlockSpec((3, tn), lambda i, j, l: (0, nj + j)),
            pl.BlockSpec((1, tn), lambda i, j, l: (0, j)),
            pl.BlockSpec((1, tn), lambda i, j, l: (0, nj + j)),
        ],
        out_specs=pl.BlockSpec((tm, tn), lambda i, j, l: (i, j)),
        out_shape=jax.ShapeDtypeStruct((m, f), BF16),
        scratch_shapes=[pltpu.VMEM((tm + 2 * hb, k), BF16)],
        compiler_params=_cparams("parallel", "arbitrary"),
        name="ffn_up_conv_glu",
    )(a, a, a, w_up, w_up, conv_w, conv_w, conv_b.reshape(1, f2), conv_b.reshape(1, f2))


def _rope_tables(n_tokens):
    rows = n_tokens // GRID_W
    r, col = jnp.meshgrid(jnp.arange(rows), jnp.arange(GRID_W), indexing="ij")
    pos = jnp.stack([r.reshape(-1), col.reshape(-1)], axis=-1).astype(F32)
    n_freq = A_DIM // 4
    inv_freq = ROPE_BASE ** (-jnp.arange(n_freq, dtype=F32) / n_freq)
    ang = pos[:, :, None, None] * inv_freq
    ang = jnp.broadcast_to(ang, (n_tokens, 2, 2, n_freq)).reshape(n_tokens, A_DIM)
    sign = jnp.where((jnp.arange(A_DIM) % (A_DIM // 2)) < (A_DIM // 4), -1.0, 1.0).astype(F32)
    return jnp.cos(ang), jnp.sin(ang) * sign


def _mixer_and_ffn(st_l, st_c, cos, sin_s, x_l, x_c, wts, stacked):
    mods = wts["mods"]
    layer = wts["layer"].reshape(1)
    d = x_l.shape[1]
    ctx_feeds_next = wts["layer"] < stacked["w_in"].shape[0] - 1

    sections = _proj_sections(d)
    kv_names = ("ka", "va", "kb", "vb")
    q_names = tuple(n for n in sections if n not in kv_names)
    h_l = _norm_mod(st_l, x_l, wts["norm1_g"], mods, 0)
    h_c = _norm_mod(st_c, x_c, wts["norm1_g"], mods, 0)
    pl_ = {n: _proj(st_l, layer, h_l, stacked["w_in"], sections[n], cos, sin_s) for n in sections}
    pc_ = {n: _proj(st_c, layer, h_c, stacked["w_in"], sections[n], cos, sin_s) for n in kv_names}
    pc_.update(zip(q_names, lax.cond(
        ctx_feeds_next,
        lambda h: tuple(_proj(st_c, layer, h, stacked["w_in"], sections[n], cos, sin_s) for n in q_names),
        lambda h: tuple(jnp.zeros((st_c.rows, sections[n][1]), BF16) for n in q_names),
        h_c)))

    oa_l = _attention(st_l, pl_["qa"], pc_["ka"], pc_["va"], pl_["ka"], pl_["va"], wts["scal"], wts["attn_subln"])
    ob_c, ob_l = _retention(st_c, st_l, tuple(pc_[n] for n in ("qb", "kb", "vb", "gb")),
                            tuple(pl_[n] for n in ("qb", "kb", "vb", "gb")),
                            wts["log_f"], wts["log_b"], wts["ret_subln"])
    oc_l = _fourier(st_l, pl_["uc"])

    def merge_and_ffn(st, x, oa, ob, oc, pg):
        y = _merge(st, layer, oa, ob, oc, stacked["w_pa"], stacked["w_pb"], stacked["w_pc"], pg)
        x1 = _res_matmul(st, layer, y, stacked["w_out"], x, mods, 2)
        h2 = _norm_mod(st, x1, wts["norm2_g"], mods, 3)
        act = _ffn_glu(st, layer, h2, stacked["w_up"], wts["conv_w"], wts["conv_b"])
        return _res_matmul(st, layer, act, stacked["w_down"], x1, mods, 5)

    def ctx_update(x):
        oa_c = _attention(st_c, pc_["qa"], pc_["ka"], pc_["va"], None, None, wts["scal"], wts["attn_subln"])
        return merge_and_ffn(st_c, x, oa_c, ob_c, _fourier(st_c, pc_["uc"]), pc_["gates"])

    x_c_new = lax.cond(ctx_feeds_next, ctx_update, lambda x: x, x_c)
    return merge_and_ffn(st_l, x_l, oa_l, ob_l, oc_l, pl_["gates"]), x_c_new


def kernel(x, c, ctx, c_ctx, w_ada_down, w_ada_up, b_ada, norm1_g, norm2_g, w_in, lam_q1, lam_k1, lam_q2,
           lam_k2, attn_subln, ret_decay_fwd, ret_decay_bwd, ret_subln, w_pa, w_pb, w_pc, w_out, w_up,
           conv_w, conv_b, w_down, final_g):
    batch, seq, d = x.shape
    lc = ctx.shape[1]
    depth = w_in.shape[0]
    assert batch + 1 <= V7X_MOD_ROWS
    st_l = _Stream(batch, seq, False)
    st_c = _Stream(batch, lc, True)

    cond = jnp.zeros((V7X_MOD_ROWS, d), F32).at[:batch].set(c.astype(F32)).at[batch].set(c_ctx.astype(F32))
    mods = _ada_params(cond, w_ada_down, w_ada_up, b_ada)

    lam_init = jnp.asarray([0.8 - 0.6 * math.exp(-0.3 * l) for l in range(depth)], F32)
    lam = (jnp.exp(jnp.sum(lam_q1.astype(F32) * lam_k1.astype(F32), axis=-1))
           - jnp.exp(jnp.sum(lam_q2.astype(F32) * lam_k2.astype(F32), axis=-1)) + lam_init)
    scal = jnp.stack([lam, 1.0 - lam_init], axis=-1)
    log_f = jax.nn.log_sigmoid(ret_decay_fwd.astype(F32))
    log_b = jax.nn.log_sigmoid(ret_decay_bwd.astype(F32))
    cos, sin_s = _rope_tables(seq)

    stacked = dict(w_in=w_in.astype(BF16), w_pa=w_pa.astype(BF16), w_pb=w_pb.astype(BF16),
                   w_pc=w_pc.astype(BF16), w_out=w_out.astype(BF16), w_up=w_up.astype(BF16),
                   w_down=w_down.astype(BF16))
    xs = dict(layer=jnp.arange(depth, dtype=jnp.int32), mods=mods, scal=scal, log_f=log_f, log_b=log_b,
              norm1_g=norm1_g, norm2_g=norm2_g, attn_subln=attn_subln, ret_subln=ret_subln,
              conv_w=conv_w, conv_b=conv_b)

    def layer(carry, wts):
        x_l, x_c = carry
        return _mixer_and_ffn(st_l, st_c, cos, sin_s, x_l, x_c, wts, stacked), None

    (x_l, _), _ = lax.scan(layer, (x.reshape(batch * seq, d), ctx.reshape(batch * lc, d)), xs)
    return _final_norm(x_l, final_g).reshape(batch, seq, d)
```
